```python
import jax, jax.numpy as jnp
from jax import lax
import numpy as np

D_MODEL = 1024
BATCH = 32
SEQ = 2048
DEPTH = 2

CHUNK = 64
CONV_CH = 512
CONV_K = 31
RET_HEADS = 4
RET_DK = 64
RET_DV = 128
RET_THETA = 10000.0
FOX_HEADS = 8
FOX_DH = 64
FOX_QBLOCK = 128
A_VAL = 0
A_GATE = A_VAL + CONV_CH
R_Q = A_GATE + CONV_CH
R_K = R_Q + RET_HEADS * RET_DK
R_V = R_K + RET_HEADS * RET_DK
R_G = R_V + RET_HEADS * RET_DV
F_Q = R_G + RET_HEADS * RET_DV
F_K = F_Q + FOX_HEADS * FOX_DH
F_V = F_K + FOX_HEADS * FOX_DH
F_F = F_V + FOX_HEADS * FOX_DH
IN_COLS = F_F + FOX_HEADS
N_BRANCH = 3
N_EXPERTS = 16
N_GROUPS = 4
EXPERTS_PER_GROUP = N_EXPERTS // N_GROUPS
TOP_K = 2
D_FF = 2048
MOE_BLOCK = 128
DN_ALPHA = (2 * DEPTH) ** 0.25
DN_BETA = (8 * DEPTH) ** -0.25
LN_EPS = 1e-5

kernel_name = "chunk_causal_hybrid_conv_retention_fox_grouped_moe"


def layer_norm(x, g=None, b=None):
    xf = x.astype(jnp.float32)
    mu = jnp.mean(xf, axis=-1, keepdims=True)
    var = jnp.mean(jnp.square(xf - mu), axis=-1, keepdims=True)
    y = (xf - mu) * lax.rsqrt(var + LN_EPS)
    if g is not None:
        y = y * g.astype(jnp.float32) + b.astype(jnp.float32)
    return y.astype(x.dtype)


def rotary(x):
    S, d = x.shape[1], x.shape[-1]
    half = d // 2
    inv = RET_THETA ** (-jnp.arange(half, dtype=jnp.float32) / half)
    ang = jnp.arange(S, dtype=jnp.float32)[:, None] * inv[None, :]
    cos = jnp.cos(ang)[None, :, None, :]
    sin = jnp.sin(ang)[None, :, None, :]
    x1, x2 = x[..., :half], x[..., half:]
    return jnp.concatenate([x1 * cos - x2 * sin, x1 * sin + x2 * cos], axis=-1)


def conv_module(a_val, a_gate, conv_w, conv_b, ln_g, ln_b):
    h = a_val * jax.nn.sigmoid(a_gate)
    h = lax.conv_general_dilated(
        h, conv_w[:, None, :].astype(h.dtype), window_strides=(1,),
        padding=[(CONV_K - 1, 0)], dimension_numbers=('NWC', 'WIO', 'NWC'),
        feature_group_count=CONV_CH) + conv_b.astype(h.dtype)
    return jax.nn.silu(layer_norm(h, ln_g, ln_b))


def retention(q, k, v):
    B, S, H, dk = q.shape
    dv = v.shape[-1]
    nc = S // CHUNK
    log_gamma = jnp.log1p(-jnp.exp2(-5.0 - jnp.arange(H, dtype=jnp.float32)))
    j = jnp.arange(CHUNK, dtype=jnp.float32)
    intra_decay = jnp.exp(log_gamma[:, None, None] * jnp.abs(j[:, None] - j[None, :]))
    q_decay = jnp.exp(log_gamma[None, :] * (j[:, None] + 1.0))[None, :, :, None]
    k_decay = jnp.exp(log_gamma[None, :] * (CHUNK - 1.0 - j[:, None]))[None, :, :, None]
    chunk_decay = jnp.exp(log_gamma * CHUNK)[None, :, None, None]
    qc = q.reshape(B, nc, CHUNK, H, dk)
    kc = k.reshape(B, nc, CHUNK, H, dk)
    vc = v.reshape(B, nc, CHUNK, H, dv)
    scores = jnp.einsum('bnjhd,bnlhd->bnhjl', qc, kc) * intra_decay
    intra = jnp.einsum('bnhjl,bnlhe->bnjhe', scores, vc)

    def step(state, inp):
        qn, kn, vn = inp
        cross = jnp.einsum('bjhd,bhde->bjhe', qn, state) * q_decay
        state = state * chunk_decay + jnp.einsum('bjhd,bjhe->bhde', kn * k_decay, vn)
        return state, cross

    state0 = jnp.zeros((B, H, dk, dv), q.dtype)
    _, cross = lax.scan(step, state0, (qc.transpose(1, 0, 2, 3, 4), kc.transpose(1, 0, 2, 3, 4),
                                        vc.transpose(1, 0, 2, 3, 4)))
    cross = cross.transpose(1, 0, 2, 3, 4)
    return (intra + cross).reshape(B, S, H, dv)


def forgetting_attention(q, k, v, f_logit):
    B, S, H, dh = q.shape
    log_f = jax.nn.log_sigmoid(f_logit.astype(jnp.float32))
    cum = jnp.cumsum(log_f, axis=1).transpose(0, 2, 1)
    scale = dh ** -0.5
    pos = jnp.arange(S)
    outs = []
    for i in range(S // FOX_QBLOCK):
        q0, q1 = i * FOX_QBLOCK, (i + 1) * FOX_QBLOCK
        logits = jnp.einsum('bqhd,bkhd->bhqk', q[:, q0:q1], k[:, :q1]).astype(jnp.float32) * scale
        logits = logits + cum[:, :, q0:q1, None] - cum[:, :, None, :q1]
        mask = pos[q0:q1, None] >= pos[None, :q1]
        logits = jnp.where(mask, logits, -jnp.inf)
        p = jax.nn.softmax(logits, axis=-1).astype(v.dtype)
        outs.append(jnp.einsum('bhqk,bkhd->bqhd', p, v[:, :q1]))
    return jnp.concatenate(outs, axis=1)


def route(xt, w_router, b_router):
    T = xt.shape[0]
    s = jax.nn.sigmoid(xt.astype(jnp.float32) @ w_router.astype(jnp.float32))
    sel = (s + b_router.astype(jnp.float32)).reshape(T, N_GROUPS, EXPERTS_PER_GROUP)
    group_score = jnp.sum(lax.top_k(sel, TOP_K)[0], axis=-1)
    g = jnp.argmax(group_score, axis=-1)
    sel_g = jnp.take_along_axis(sel, g[:, None, None], axis=1)[:, 0]
    _, local = lax.top_k(sel_g, TOP_K)
    idx = g[:, None] * EXPERTS_PER_GROUP + local
    w = jnp.take_along_axis(s, idx, axis=1)
    return idx, w / jnp.sum(w, axis=-1, keepdims=True)


def moe_ffn(u, w_router, b_router, w1, w3, w2):
    B, S, D = u.shape
    T = B * S
    xt = u.reshape(T, D)
    idx, wts = route(xt, w_router, b_router)
    A = T * TOP_K
    flat_e = idx.reshape(A)
    flat_w = wts.reshape(A)
    flat_tok = jnp.repeat(jnp.arange(T, dtype=jnp.int32), TOP_K)
    order = jnp.argsort(flat_e)
    se = flat_e[order]
    counts = jnp.bincount(flat_e, length=N_EXPERTS)
    start = jnp.cumsum(counts) - counts
    padded = (counts + MOE_BLOCK - 1) // MOE_BLOCK * MOE_BLOCK
    pend = jnp.cumsum(padded)
    pstart = pend - padded
    dest = pstart[se] + jnp.arange(A) - start[se]
    P = A + N_EXPERTS * MOE_BLOCK
    NB = P // MOE_BLOCK
    buf_tok = jnp.full((P,), T, jnp.int32).at[dest].set(flat_tok[order])
    buf_w = jnp.zeros((P,), jnp.float32).at[dest].set(flat_w[order])
    block_e = jnp.minimum(jnp.searchsorted(pend, jnp.arange(NB) * MOE_BLOCK, side='right'), N_EXPERTS - 1)
    x_pad = jnp.concatenate([xt, jnp.zeros((1, D), xt.dtype)], axis=0)
    xb = x_pad[buf_tok].reshape(NB, MOE_BLOCK, D)

    def expert_block(args):
        xblk, e = args
        h = jax.nn.silu(xblk @ w1[e]) * (xblk @ w3[e])
        return h @ w2[e]

    yb = lax.map(expert_block, (xb, block_e)).reshape(P, D)
    out = jnp.zeros((T + 1, D), yb.dtype).at[buf_tok].add(yb * buf_w[:, None].astype(yb.dtype))
    return out[:T].reshape(B, S, D)


def setup_inputs(seed: int = 0) -> dict:
    key = jax.random.key(seed)
    ks = jax.random.split(key, 32)
    f32 = jnp.float32
    L, D = DEPTH, D_MODEL

    def nrm(k, shape, fan_in, gain=1.0):
        return jax.random.normal(k, shape, f32) * (gain * fan_in ** -0.5)

    def small(k, shape, s=0.02):
        return jax.random.normal(k, shape, f32) * s

    return {
        "x": jax.random.normal(ks[0], (BATCH, SEQ, D), f32),
        "c": jax.random.normal(ks[1], (BATCH, D), f32),
        "w_ada": nrm(ks[2], (L, D, 6 * D), D, 0.5),
        "b_ada": small(ks[3], (L, 6 * D)),
        "w_in": nrm(ks[4], (L, D, IN_COLS), D),
        "conv_w": nrm(ks[5], (L, CONV_K, CONV_CH), CONV_K),
        "conv_b": small(ks[6], (L, CONV_CH)),
        "conv_ln_g": 1.0 + small(ks[7], (L, CONV_CH)),
        "conv_ln_b": small(ks[8], (L, CONV_CH)),
        "b_forget": jax.random.uniform(ks[9], (L, FOX_HEADS), f32, 1.0, 5.0),
        "w_conv_out": nrm(ks[10], (L, CONV_CH, D), CONV_CH),
        "w_ret_out": nrm(ks[11], (L, RET_HEADS * RET_DV, D), RET_HEADS * RET_DV),
        "w_fox_out": nrm(ks[12], (L, FOX_HEADS * FOX_DH, D), FOX_HEADS * FOX_DH),
        "w_gate": nrm(ks[13], (L, D, N_BRANCH * D), D),
        "b_gate": small(ks[14], (L, N_BRANCH * D)),
        "w_out": nrm(ks[15], (L, D, D), D, DN_BETA),
        "ln1_g": 1.0 + small(ks[16], (L, D)),
        "ln1_b": small(ks[17], (L, D)),
        "w_router": nrm(ks[18], (D, N_EXPERTS), D),
        "b_router": small(ks[19], (N_EXPERTS,), 0.01),
        "w1": nrm(ks[20], (L, N_EXPERTS, D, D_FF), D),
        "w3": nrm(ks[21], (L, N_EXPERTS, D, D_FF), D),
        "w2": nrm(ks[22], (L, N_EXPERTS, D_FF, D), D_FF, DN_BETA),
        "ln2_g": 1.0 + small(ks[23], (L, D)),
        "ln2_b": small(ks[24], (L, D)),
    }


def reference(x, c, w_ada, b_ada, w_in, conv_w, conv_b, conv_ln_g, conv_ln_b, b_forget,
              w_conv_out, w_ret_out, w_fox_out, w_gate, b_gate, w_out, ln1_g, ln1_b,
              w_router, b_router, w1, w3, w2, ln2_g, ln2_b):
    B, S, D = x.shape
    for l in range(DEPTH):
        ada = jax.nn.silu(c) @ w_ada[l] + b_ada[l]
        sh1, sc1, g1, sh2, sc2, g2 = jnp.split(ada[:, None, :], 6, axis=-1)

        u = layer_norm(x) * (1.0 + sc1) + sh1
        z = u @ w_in[l]

        y_a = conv_module(z[..., A_VAL:A_GATE], z[..., A_GATE:R_Q], conv_w[l], conv_b[l],
                          conv_ln_g[l], conv_ln_b[l]) @ w_conv_out[l]

        rq = rotary(z[..., R_Q:R_K].reshape(B, S, RET_HEADS, RET_DK).astype(jnp.float32))
        rk = rotary(z[..., R_K:R_V].reshape(B, S, RET_HEADS, RET_DK).astype(jnp.float32)) * (RET_DK ** -0.5)
        rv = z[..., R_V:R_G].reshape(B, S, RET_HEADS, RET_DV).astype(jnp.float32)
        ry = layer_norm(retention(rq, rk, rv)).reshape(B, S, RET_HEADS * RET_DV).astype(z.dtype)
        y_b = (jax.nn.silu(z[..., R_G:F_Q]) * ry) @ w_ret_out[l]

        fq = z[..., F_Q:F_K].reshape(B, S, FOX_HEADS, FOX_DH)
        fk = z[..., F_K:F_V].reshape(B, S, FOX_HEADS, FOX_DH)
        fv = z[..., F_V:F_F].reshape(B, S, FOX_HEADS, FOX_DH)
        f_logit = z[..., F_F:IN_COLS] + b_forget[l]
        y_c = forgetting_attention(fq, fk, fv, f_logit).reshape(B, S, FOX_HEADS * FOX_DH) @ w_fox_out[l]

        ga, gb, gc = jnp.split(jax.nn.sigmoid(u @ w_gate[l] + b_gate[l]), N_BRANCH, axis=-1)
        h = (ga * y_a + gb * y_b + gc * y_c) @ w_out[l]
        x = layer_norm(DN_ALPHA * x + g1 * h, ln1_g[l], ln1_b[l])

        u2 = layer_norm(x) * (1.0 + sc2) + sh2
        h2 = moe_ffn(u2, w_router, b_router, w1[l], w3[l], w2[l])
        x = layer_norm(DN_ALPHA * x + g2 * h2, ln2_g[l], ln2_b[l])
    return x
```

```python
import functools

import jax
import jax.numpy as jnp
import numpy as np
from jax import lax
from jax.experimental import pallas as pl
from jax.experimental.pallas import tpu as pltpu

F32 = jnp.float32
BF16 = jnp.bfloat16

CHUNK = 64
CONV_CH = 512
CONV_K = 31
RET_HEADS = 4
RET_DK = 64
RET_DV = 128
RET_THETA = 10000.0
FOX_HEADS = 8
FOX_DH = 64
N_EXPERTS = 16
N_GROUPS = 4
EXPERTS_PER_GROUP = 4
TOP_K = 2
LN_EPS = 1e-5

A_VAL = 0
A_GATE = 512
R_Q = 1024
R_K = 1280
R_V = 1536
R_G = 2048
F_Q = 2560
F_K = 3072
F_V = 3584
F_F = 4096
IN_COLS_PAD = 4224

LANES = 128
CONV_HALO = 32

TM_PROJ = 512
TS_CONV = 512
TS_RET = 256
TQ_FOX = 256
TM_MOE = 512
FF_CHUNK = 512
TM_COMB = 256

VMEM_LIMIT = 56 * 1024 * 1024


def _cparams(sem, vmem=VMEM_LIMIT):
    return pltpu.CompilerParams(dimension_semantics=sem, vmem_limit_bytes=vmem)


def _ln(x):
    mu = jnp.mean(x, axis=-1, keepdims=True)
    xc = x - mu
    var = jnp.mean(xc * xc, axis=-1, keepdims=True)
    return xc * lax.rsqrt(var + LN_EPS)


def _sigmoid(x):
    return 1.0 / (1.0 + jnp.exp(-x))


def _silu(x):
    return x * _sigmoid(x)


def _split2(x):
    hi = x.astype(BF16)
    lo = (x - hi.astype(F32)).astype(BF16)
    return hi, lo


def _split3(x):
    c1 = x.astype(BF16).astype(F32)
    r = x - c1
    c2 = r.astype(BF16).astype(F32)
    c3 = (r - c2).astype(BF16).astype(F32)
    return c1, c2, c3


def _dot(a, b):
    return jnp.dot(a, b, preferred_element_type=F32)


def _dot_nt(a, b):
    return lax.dot_general(a, b, (((1,), (1,)), ((), ())), preferred_element_type=F32)


def _dot_tn(a, b):
    return lax.dot_general(a, b, (((0,), (0,)), ((), ())), preferred_element_type=F32)


def _ada_kernel(c_ref, w_ref, b_ref, o_ref):
    sc = _silu(c_ref[...])
    a_hi, a_lo = _split2(sc)
    w = w_ref[0]
    w_hi, w_lo = _split2(w)
    o_ref[0] = _dot(a_hi, w_hi) + _dot(a_lo, w_hi) + _dot(a_hi, w_lo) + b_ref[0]


def _ada(c, w_ada, b_ada):
    L, D, N = w_ada.shape
    B = c.shape[0]
    tn = 1536
    return pl.pallas_call(
        _ada_kernel,
        grid=(L, N // tn),
        in_specs=[
            pl.BlockSpec((B, D), lambda l, j: (0, 0)),
            pl.BlockSpec((1, D, tn), lambda l, j: (l, 0, j)),
            pl.BlockSpec((1, 1, tn), lambda l, j: (l, 0, j)),
        ],
        out_specs=pl.BlockSpec((1, B, tn), lambda l, j: (l, 0, j)),
        out_shape=jax.ShapeDtypeStruct((L, B, N), F32),
        compiler_params=_cparams(("arbitrary", "arbitrary")),
        name="ada",
    )(c, w_ada, b_ada.reshape(L, 1, N))


def _rotary(z, cos, sin_signed, lane):
    partner = jnp.where((lane & 63) < 32, pltpu.roll(z, 96, axis=1), pltpu.roll(z, 32, axis=1))
    return z * cos + partner * sin_signed


def _inproj_kernel(x_ref, ada_ref, w_ref, cos_ref, sin_ref, bf_ref,
                   h_ref, rq_ref, rk_ref, rv_ref, rg_ref, fq_ref, fk_ref, fv_ref, carry_ref):
    s = pl.program_id(1)
    tm = x_ref.shape[1]

    @pl.when(s == 0)
    def _():
        carry_ref[...] = jnp.zeros_like(carry_ref)

    x = x_ref[0]
    sh1 = ada_ref[0, 0:1, :]
    sc1 = ada_ref[0, 1:2, :]
    u = (_ln(x) * (1.0 + sc1) + sh1).astype(BF16)

    def proj(c0, c1):
        return _dot(u, w_ref[:, c0:c1])

    h_ref[0] = (proj(A_VAL, A_GATE) * _sigmoid(proj(A_GATE, R_Q))).astype(BF16)

    lane = lax.broadcasted_iota(jnp.int32, (tm, LANES), 1)
    cos = cos_ref[...]
    sin = sin_ref[...]
    for p in range(2):
        c0 = R_Q + p * LANES
        rq_ref[0, :, p * LANES:(p + 1) * LANES] = _rotary(proj(c0, c0 + LANES), cos, sin, lane).astype(BF16)
        c0 = R_K + p * LANES
        rk = _rotary(proj(c0, c0 + LANES), cos, sin, lane) * (RET_DK ** -0.5)
        rk_ref[0, :, p * LANES:(p + 1) * LANES] = rk.astype(BF16)
    rv_ref[0] = proj(R_V, R_G).astype(BF16)
    rg_ref[0] = _silu(proj(R_G, F_Q)).astype(BF16)
    fv_ref[0] = proj(F_V, F_F).astype(BF16)

    f_logit = proj(F_F, IN_COLS_PAD) + bf_ref[...]
    log_f = jnp.minimum(f_logit, 0.0) - jnp.log(1.0 + jnp.exp(-jnp.abs(f_logit)))
    row = lax.broadcasted_iota(jnp.int32, (tm, LANES), 0)
    y = log_f
    sh = 1
    while sh < tm:
        y = y + jnp.where(row >= sh, pltpu.roll(y, sh, axis=0), 0.0)
        sh *= 2
    cum = y + carry_ref[0:1, :]
    carry_ref[0:1, :] = cum[tm - 1:tm, :]
    c1, c2, c3 = _split3(cum)

    one_q = jnp.where((lane >= 67) & (lane < 70), 1.0, 0.0)
    one_k = jnp.where((lane >= 64) & (lane < 67), 1.0, 0.0)
    for p in range(FOX_HEADS // 2):
        zq = proj(F_Q + p * LANES, F_Q + (p + 1) * LANES) * (FOX_DH ** -0.5)
        zk = proj(F_K + p * LANES, F_K + (p + 1) * LANES)
        zq_sw = pltpu.roll(zq, 64, axis=1)
        zk_sw = pltpu.roll(zk, 64, axis=1)
        for hh in range(2):
            h = 2 * p + hh
            a1 = c1[:, h:h + 1]
            a2 = c2[:, h:h + 1]
            a3 = c3[:, h:h + 1]
            ext_q = jnp.where(lane == 64, a1, jnp.where(lane == 65, a2, jnp.where(lane == 66, a3, one_q)))
            ext_k = jnp.where(lane == 67, -a1, jnp.where(lane == 68, -a2, jnp.where(lane == 69, -a3, one_k)))
            fq_ref[0, h] = jnp.where(lane < 64, zq if hh == 0 else zq_sw, ext_q).astype(BF16)
            fk_ref[0, h] = jnp.where(lane < 64, zk if hh == 0 else zk_sw, ext_k).astype(BF16)


def _inproj(x, ada_l, w_in_b, cos_t, sin_t, bf_pad):
    B, S, D = x.shape
    tm = min(TM_PROJ, S)
    bsd = lambda n, dt: jax.ShapeDtypeStruct((B, S, n), dt)
    row_spec = lambda n: pl.BlockSpec((1, tm, n), lambda b, s: (b, s, 0))
    aug_spec = pl.BlockSpec((1, FOX_HEADS, tm, LANES), lambda b, s: (b, 0, s, 0))
    aug_shape = jax.ShapeDtypeStruct((B, FOX_HEADS, S, LANES), BF16)
    return pl.pallas_call(
        _inproj_kernel,
        grid=(B, S // tm),
        in_specs=[
            row_spec(D),
            pl.BlockSpec((1, 6, D), lambda b, s: (b, 0, 0)),
            pl.BlockSpec((D, IN_COLS_PAD), lambda b, s: (0, 0)),
            pl.BlockSpec((tm, LANES), lambda b, s: (s, 0)),
            pl.BlockSpec((tm, LANES), lambda b, s: (s, 0)),
            pl.BlockSpec((1, LANES), lambda b, s: (0, 0)),
        ],
        out_specs=[row_spec(CONV_CH), row_spec(256), row_spec(256), row_spec(512), row_spec(512),
                   aug_spec, aug_spec, row_spec(512)],
        out_shape=[bsd(CONV_CH, BF16), bsd(256, BF16), bsd(256, BF16), bsd(512, BF16), bsd(512, BF16),
                   aug_shape, aug_shape, bsd(512, BF16)],
        scratch_shapes=[pltpu.VMEM((8, LANES), F32)],
        compiler_params=_cparams(("arbitrary", "arbitrary")),
        name="inproj",
    )(x, ada_l, w_in_b, cos_t, sin_t, bf_pad)


def _conv_kernel(h_ref, w_ref, b_ref, g_ref, beta_ref, o_ref, hbuf):
    s = pl.program_id(1)
    ts = h_ref.shape[1]

    @pl.when(s == 0)
    def _():
        hbuf[0:CONV_HALO, :] = jnp.zeros((CONV_HALO, CONV_CH), F32)

    hbuf[CONV_HALO:CONV_HALO + ts, :] = h_ref[0].astype(F32)
    acc = jnp.zeros((ts, CONV_CH), F32) + b_ref[...]
    off = CONV_HALO - (CONV_K - 1)
    for k in range(CONV_K):
        acc = acc + w_ref[k:k + 1, :] * hbuf[off + k:off + k + ts, :]
    y = _ln(acc) * g_ref[...] + beta_ref[...]
    o_ref[0] = _silu(y).astype(BF16)
    hbuf[0:CONV_HALO, :] = hbuf[ts:ts + CONV_HALO, :]


def _conv(h, conv_w, conv_b, g, beta):
    B, S, C = h.shape
    ts = min(TS_CONV, S)
    w_pad = jnp.zeros((32, C), F32).at[:CONV_K].set(conv_w)
    vec = lambda: pl.BlockSpec((1, C), lambda b, s: (0, 0))
    return pl.pallas_call(
        _conv_kernel,
        grid=(B, S // ts),
        in_specs=[pl.BlockSpec((1, ts, C), lambda b, s: (b, s, 0)),
                  pl.BlockSpec((32, C), lambda b, s: (0, 0)), vec(), vec(), vec()],
        out_specs=pl.BlockSpec((1, ts, C), lambda b, s: (b, s, 0)),
        out_shape=jax.ShapeDtypeStruct((B, S, C), BF16),
        scratch_shapes=[pltpu.VMEM((ts + CONV_HALO, C), F32)],
        compiler_params=_cparams(("arbitrary", "arbitrary")),
        name="conv",
    )(h, w_pad, conv_b.reshape(1, C), g.reshape(1, C), beta.reshape(1, C))


def _ret_kernel(q_ref, k_ref, v_ref, g_ref, mask_ref, qdec_ref, kdec_ref, sdec_ref, o_ref, state):
    s = pl.program_id(1)
    ts = q_ref.shape[1]

    @pl.when(s == 0)
    def _():
        state[...] = jnp.zeros_like(state)

    lane = lax.broadcasted_iota(jnp.int32, (ts, LANES), 1)
    for p in range(RET_HEADS // 2):
        q = q_ref[0, :, p * LANES:(p + 1) * LANES]
        kp = k_ref[0, :, p * LANES:(p + 1) * LANES]
        for hh in range(2):
            h = 2 * p + hh
            own = (lane < 64) if hh == 0 else (lane >= 64)
            k = jnp.where(own, kp, jnp.zeros_like(kp))
            v = v_ref[0, :, h * RET_DV:(h + 1) * RET_DV]
            sc = _dot_nt(q, k) * mask_ref[h]
            st = state[h]
            o = _dot(sc.astype(BF16), v) + _dot(q, st.astype(BF16)) * qdec_ref[h]
            kd = (k.astype(F32) * kdec_ref[h]).astype(BF16)
            state[h] = st * sdec_ref[h] + _dot_tn(kd, v)
            ry = _ln(o)
            o_ref[0, :, h * RET_DV:(h + 1) * RET_DV] = (g_ref[0, :, h * RET_DV:(h + 1) * RET_DV].astype(F32) * ry).astype(BF16)


def _ret_tables(ts):
    log_gamma = jnp.log1p(-jnp.exp2(-5.0 - jnp.arange(RET_HEADS, dtype=F32)))
    t = jnp.arange(ts)
    dt = (t[:, None] - t[None, :]).astype(F32)
    same = (t[:, None] // CHUNK) == (t[None, :] // CHUNK)
    earlier = (t[None, :] // CHUNK) < (t[:, None] // CHUNK)
    expo = jnp.where(same, jnp.abs(dt), dt)
    w = jnp.exp(log_gamma[:, None, None] * expo[None])
    mask = jnp.where((same | earlier)[None], w, 0.0)
    tf = t.astype(F32)
    qdec = jnp.broadcast_to(jnp.exp(log_gamma[:, None] * (tf[None, :] + 1.0))[:, :, None], (RET_HEADS, ts, RET_DV))
    kdec = jnp.broadcast_to(jnp.exp(log_gamma[:, None] * (ts - 1.0 - tf[None, :]))[:, :, None], (RET_HEADS, ts, LANES))
    sdec = jnp.broadcast_to(jnp.exp(log_gamma * ts)[:, None, None], (RET_HEADS, LANES, RET_DV))
    return mask.astype(F32), qdec.astype(F32), kdec.astype(F32), sdec.astype(F32)


def _retention(rq, rk, rv, rg):
    B, S, _ = rq.shape
    ts = min(TS_RET, S)
    mask, qdec, kdec, sdec = _ret_tables(ts)
    row = lambda n: pl.BlockSpec((1, ts, n), lambda b, s: (b, s, 0))
    full = lambda a: pl.BlockSpec(a.shape, lambda b, s: (0, 0, 0))
    return pl.pallas_call(
        _ret_kernel,
        grid=(B, S // ts),
        in_specs=[row(256), row(256), row(512), row(512), full(mask), full(qdec), full(kdec), full(sdec)],
        out_specs=row(512),
        out_shape=jax.ShapeDtypeStruct((B, S, 512), BF16),
        scratch_shapes=[pltpu.VMEM((RET_HEADS, LANES, RET_DV), F32)],
        compiler_params=_cparams(("arbitrary", "arbitrary")),
        name="retention",
    )(rq, rk, rv, rg, mask, qdec, kdec, sdec)


def _fox_kernel(q_ref, k_ref, v_ref, o_ref):
    qi = pl.program_id(2)
    tq = q_ref.shape[2]
    tk = tq
    lane = lax.broadcasted_iota(jnp.int32, (tq, LANES), 1)
    outs = []
    for hh in range(2):
        q = q_ref[0, hh]

        def step(j, carry, masked):
            m, l, acc = carry
            k = k_ref[0, hh, pl.ds(pl.multiple_of(j * tk, tk), tk), :]
            v = v_ref[0, pl.ds(pl.multiple_of(j * tk, tk), tk), :]
            s = _dot_nt(q, k)
            if masked:
                r = lax.broadcasted_iota(jnp.int32, (tq, tk), 0)
                c = lax.broadcasted_iota(jnp.int32, (tq, tk), 1)
                s = jnp.where(r >= c, s, -1e30)
            m_new = jnp.maximum(m, jnp.max(s, axis=-1, keepdims=True))
            p = jnp.exp(s - m_new)
            alpha = jnp.exp(m - m_new)
            l = alpha * l + jnp.sum(p, axis=-1, keepdims=True)
            acc = alpha * acc + _dot(p.astype(BF16), v)
            return m_new, l, acc

        init = (jnp.full((tq, 1), -1e30, F32), jnp.zeros((tq, 1), F32), jnp.zeros((tq, LANES), F32))
        carry = lax.fori_loop(0, qi, functools.partial(step, masked=False), init)
        m, l, acc = step(qi, carry, True)
        outs.append(acc / l)
    o_ref[0] = jnp.where(lane < 64, outs[0], outs[1]).astype(BF16)


def _fox(fq, fk, fv):
    B, H, S, _ = fq.shape
    tq = min(TQ_FOX, S)
    return pl.pallas_call(
        _fox_kernel,
        grid=(B, H // 2, S // tq),
        in_specs=[
            pl.BlockSpec((1, 2, tq, LANES), lambda b, p, i: (b, p, i, 0)),
            pl.BlockSpec((1, 2, S, LANES), lambda b, p, i: (b, p, 0, 0)),
            pl.BlockSpec((1, S, LANES), lambda b, p, i: (b, 0, p)),
        ],
        out_specs=pl.BlockSpec((1, tq, LANES), lambda b, p, i: (b, i, p)),
        out_shape=jax.ShapeDtypeStruct((B, S, H * FOX_DH), BF16),
        compiler_params=_cparams(("arbitrary", "arbitrary", "arbitrary")),
        name="fox",
    )(fq, fk, fv)


def _top2_of4(a, b, c, d):
    m1, n1 = jnp.maximum(a, b), jnp.minimum(a, b)
    m2, n2 = jnp.maximum(c, d), jnp.minimum(c, d)
    return jnp.maximum(m1, m2) + jnp.maximum(jnp.minimum(m1, m2), jnp.maximum(n1, n2))


def _route_rows(score, sel):
    gs = [_top2_of4(*sel[4 * g:4 * g + 4]) for g in range(N_GROUPS)]
    best = gs[0]
    bi = jnp.zeros_like(best, dtype=jnp.int32)
    for g in range(1, N_GROUPS):
        better = gs[g] > best
        bi = jnp.where(better, g, bi)
        best = jnp.where(better, gs[g], best)

    def pick(rows, j):
        out = rows[j]
        for g in range(1, N_GROUPS):
            out = jnp.where(bi == g, rows[4 * g + j], out)
        return out

    m = [pick(sel, j) for j in range(4)]
    sc = [pick(score, j) for j in range(4)]

    def argmax4(vals):
        bv, bj = vals[0], jnp.zeros_like(bi)
        for j in range(1, 4):
            better = vals[j] > bv
            bj = jnp.where(better, j, bj)
            bv = jnp.where(better, vals[j], bv)
        return bj

    i1 = argmax4(m)
    i2 = argmax4([jnp.where(i1 == j, -jnp.inf, m[j]) for j in range(4)])

    def take(vals, idx):
        out = vals[0]
        for j in range(1, 4):
            out = jnp.where(idx == j, vals[j], out)
        return out

    w1, w2 = take(sc, i1), take(sc, i2)
    tot = w1 + w2
    return 4 * bi + i1, 4 * bi + i2, w1 / tot, w2 / tot


def _merge_kernel(x_ref, ca_ref, ra_ref, fa_ref, ada_ref, wg_ref, bg_ref, wco_ref, wro_ref, wfo_ref, wout_ref,
                  g1_ref, b1_ref, wr_ref, br_ref, x1_ref, u2_ref, ridx_ref, rw_ref, *, alpha):
    D = x_ref.shape[1]
    x = x_ref[...]
    sh1 = ada_ref[0, 0:1, :]
    sc1 = ada_ref[0, 1:2, :]
    g1 = ada_ref[0, 2:3, :]
    sh2 = ada_ref[0, 3:4, :]
    sc2 = ada_ref[0, 4:5, :]
    u = (_ln(x) * (1.0 + sc1) + sh1).astype(BF16)
    m = None
    for i, (a_ref, w_ref) in enumerate(((ca_ref, wco_ref), (ra_ref, wro_ref), (fa_ref, wfo_ref))):
        gate = _sigmoid(_dot(u, wg_ref[:, i * D:(i + 1) * D]) + bg_ref[:, i * D:(i + 1) * D])
        t = gate * _dot(a_ref[...], w_ref[...])
        m = t if m is None else m + t
    h = _dot(m.astype(BF16), wout_ref[...])
    x1 = _ln(alpha * x + g1 * h) * g1_ref[...] + b1_ref[...]
    x1_ref[...] = x1
    u2 = _ln(x1) * (1.0 + sc2) + sh2
    u2_ref[...] = u2
    u_hi, u_lo = _split2(u2)
    logits = _dot(u_hi, wr_ref[0]) + _dot(u_lo, wr_ref[0]) + _dot(u_hi, wr_ref[1])
    lt = jnp.transpose(logits)
    score = [_sigmoid(lt[e:e + 1, :]) for e in range(N_EXPERTS)]
    sel = [score[e] + br_ref[e:e + 1, :] for e in range(N_EXPERTS)]
    e1, e2, w1, w2 = _route_rows(score, sel)
    ridx_ref[0:1, :] = e1
    ridx_ref[1:2, :] = e2
    rw_ref[0:1, :] = w1
    rw_ref[1:2, :] = w2


def _merge(x2, ca, ra, fa, ada_l, wg, bg, wco, wro, wfo, wout, g1, b1, wr2, br_b, S, alpha):
    T, D = x2.shape
    tm = min(TM_PROJ, S)
    spb = S // tm
    row = lambda n: pl.BlockSpec((tm, n), lambda i: (i, 0))
    const = lambda a: pl.BlockSpec(a.shape, lambda i: (0,) * a.ndim)
    return pl.pallas_call(
        functools.partial(_merge_kernel, alpha=alpha),
        grid=(T // tm,),
        in_specs=[row(D), row(512), row(512), row(512),
                  pl.BlockSpec((1, 6, D), lambda i: (i // spb, 0, 0)),
                  const(wg), const(bg), const(wco), const(wro), const(wfo), const(wout), const(g1), const(b1),
                  const(wr2), const(br_b)],
        out_specs=[row(D), row(D), pl.BlockSpec((2, tm), lambda i: (0, i)), pl.BlockSpec((2, tm), lambda i: (0, i))],
        out_shape=[jax.ShapeDtypeStruct((T, D), F32), jax.ShapeDtypeStruct((T, D), F32),
                   jax.ShapeDtypeStruct((2, T), jnp.int32), jax.ShapeDtypeStruct((2, T), F32)],
        compiler_params=_cparams(("arbitrary",)),
        name="merge",
    )(x2, ca, ra, fa, ada_l, wg, bg, wco, wro, wfo, wout, g1, b1, wr2, br_b)


def _moe_kernel(be_ref, nv_ref, tok_cur_ref, tok_nxt_ref, x_hbm, w1_ref, w3_ref, w2_ref, y_ref, xbuf, sem):
    i = pl.program_id(0)
    nb = pl.num_programs(0)
    tm = xbuf.shape[1]
    slot = i % 2

    def gather(tok_ref, dst_slot):
        def body(r, c):
            t = tok_ref[0, r]
            pltpu.make_async_copy(x_hbm.at[pl.ds(t, 1), :], xbuf.at[dst_slot, pl.ds(r, 1), :], sem.at[dst_slot]).start()
            return c
        lax.fori_loop(0, tm, body, 0)

    @pl.when(i == 0)
    def _():
        gather(tok_cur_ref, 0)

    @pl.when(i + 1 < nb)
    def _():
        gather(tok_nxt_ref, 1 - slot)

    pltpu.make_async_copy(x_hbm.at[pl.ds(0, tm), :], xbuf.at[slot], sem.at[slot]).wait()

    @pl.when(i < nv_ref[0])
    def _():
        x = xbuf[slot].astype(BF16)
        acc = None
        for c in range(w1_ref.shape[2] // FF_CHUNK):
            c0 = c * FF_CHUNK
            h1 = _dot(x, w1_ref[0, :, c0:c0 + FF_CHUNK])
            h3 = _dot(x, w3_ref[0, :, c0:c0 + FF_CHUNK])
            h = (_silu(h1) * h3).astype(BF16)
            t = _dot(h, w2_ref[0, c0:c0 + FF_CHUNK, :])
            acc = t if acc is None else acc + t
        y_ref[...] = acc

    @pl.when(i >= nv_ref[0])
    def _():
        y_ref[...] = jnp.zeros_like(y_ref)


def _moe(u2, buf_tok, block_e, nvalid, w1b, w3b, w2b):
    T, D = u2.shape
    E, _, FF = w1b.shape
    nb = buf_tok.shape[0]
    tm = buf_tok.shape[2]
    grid_spec = pltpu.PrefetchScalarGridSpec(
        num_scalar_prefetch=2,
        grid=(nb,),
        in_specs=[
            pl.BlockSpec((None, 1, tm), lambda i, be, nv: (i, 0, 0), memory_space=pltpu.SMEM),
            pl.BlockSpec((None, 1, tm), lambda i, be, nv: (jnp.minimum(i + 1, nb - 1), 0, 0), memory_space=pltpu.SMEM),
            pl.BlockSpec(memory_space=pl.ANY),
            pl.BlockSpec((1, D, FF), lambda i, be, nv: (be[i], 0, 0)),
            pl.BlockSpec((1, D, FF), lambda i, be, nv: (be[i], 0, 0)),
            pl.BlockSpec((1, FF, D), lambda i, be, nv: (be[i], 0, 0)),
        ],
        out_specs=pl.BlockSpec((tm, D), lambda i, be, nv: (i, 0)),
        scratch_shapes=[pltpu.VMEM((2, tm, D), F32), pltpu.SemaphoreType.DMA((2,))],
    )
    return pl.pallas_call(
        _moe_kernel,
        grid_spec=grid_spec,
        out_shape=jax.ShapeDtypeStruct((nb * tm, D), F32),
        compiler_params=_cparams(("arbitrary",)),
        name="moe",
    )(block_e, nvalid, buf_tok, buf_tok, u2, w1b, w3b, w2b)


def _combine_kernel(d_cur_ref, d_nxt_ref, y_hbm, x1_ref, w_ref, ada_ref, g_ref, b_ref, o_ref, ybuf, sem, *, alpha):
    i = pl.program_id(0)
    nb = pl.num_programs(0)
    tm = x1_ref.shape[0]
    slot = i % 2

    def gather(d_ref, dst_slot):
        def body(r, c):
            for k in range(TOP_K):
                p = d_ref[k, r]
                pltpu.make_async_copy(y_hbm.at[pl.ds(p, 1), :], ybuf.at[dst_slot, k, pl.ds(r, 1), :],
                                      sem.at[dst_slot]).start()
            return c
        lax.fori_loop(0, tm, body, 0)

    @pl.when(i == 0)
    def _():
        gather(d_cur_ref, 0)

    @pl.when(i + 1 < nb)
    def _():
        gather(d_nxt_ref, 1 - slot)

    for k in range(TOP_K):
        pltpu.make_async_copy(y_hbm.at[pl.ds(0, tm), :], ybuf.at[slot, k], sem.at[slot]).wait()

    g2 = ada_ref[0, 5:6, :]
    w = w_ref[...]
    h2 = w[:, 0:1] * ybuf[slot, 0] + w[:, 1:2] * ybuf[slot, 1]
    o_ref[...] = _ln(alpha * x1_ref[...] + g2 * h2) * g_ref[...] + b_ref[...]


def _combine(dest3, y_sorted, x1, wts, ada_l, g2v, b2v, S, alpha):
    T, D = x1.shape
    tm = min(TM_COMB, S)
    spb = S // tm
    nb = T // tm
    const = lambda a: pl.BlockSpec(a.shape, lambda i: (0,) * a.ndim)
    return pl.pallas_call(
        functools.partial(_combine_kernel, alpha=alpha),
        grid=(nb,),
        in_specs=[
            pl.BlockSpec((None, TOP_K, tm), lambda i: (i, 0, 0), memory_space=pltpu.SMEM),
            pl.BlockSpec((None, TOP_K, tm), lambda i: (jnp.minimum(i + 1, nb - 1), 0, 0), memory_space=pltpu.SMEM),
            pl.BlockSpec(memory_space=pl.ANY),
            pl.BlockSpec((tm, D), lambda i: (i, 0)),
            pl.BlockSpec((tm, TOP_K), lambda i: (i, 0)),
            pl.BlockSpec((1, 6, D), lambda i: (i // spb, 0, 0)),
            const(g2v), const(b2v),
        ],
        out_specs=pl.BlockSpec((tm, D), lambda i: (i, 0)),
        out_shape=jax.ShapeDtypeStruct((T, D), F32),
        scratch_shapes=[pltpu.VMEM((2, TOP_K, tm, D), F32), pltpu.SemaphoreType.DMA((2,))],
        compiler_params=_cparams(("arbitrary",)),
        name="combine",
    )(dest3, dest3, y_sorted, x1, wts, ada_l, g2v, b2v)


def _plan(ridx, tm):
    _, T = ridx.shape
    A = TOP_K * T
    flat_e = ridx.reshape(A)
    onehot = (flat_e[:, None] == jnp.arange(N_EXPERTS, dtype=jnp.int32)[None, :]).astype(jnp.int32)
    csum = jnp.cumsum(onehot, axis=0)
    counts = csum[-1]
    rank = jnp.take_along_axis(csum, flat_e[:, None], axis=1)[:, 0] - 1
    padded = (counts + tm - 1) // tm * tm
    pend = jnp.cumsum(padded)
    pstart = pend - padded
    dest = (pstart[flat_e] + rank).astype(jnp.int32)
    nb = A // tm + N_EXPERTS
    tok = jnp.tile(jnp.arange(T, dtype=jnp.int32), TOP_K)
    buf_tok = jnp.zeros((nb * tm,), jnp.int32).at[dest].set(tok)
    block_e = jnp.minimum(jnp.searchsorted(pend, jnp.arange(nb, dtype=jnp.int32) * tm, side='right'),
                          N_EXPERTS - 1).astype(jnp.int32)
    nvalid = (pend[-1] // tm).astype(jnp.int32).reshape(1)
    return dest.reshape(TOP_K, T), buf_tok.reshape(nb, 1, tm), block_e, nvalid


def _rotary_tables(S):
    half = RET_DK // 2
    inv = RET_THETA ** (-jnp.arange(half, dtype=F32) / half)
    ang = jnp.arange(S, dtype=F32)[:, None] * inv[None, :]
    cos, sin = jnp.cos(ang), jnp.sin(ang)
    cos_t = jnp.tile(jnp.concatenate([cos, cos], axis=1), (1, LANES // RET_DK))
    sin_t = jnp.tile(jnp.concatenate([-sin, sin], axis=1), (1, LANES // RET_DK))
    return cos_t, sin_t


def kernel(x, c, w_ada, b_ada, w_in, conv_w, conv_b, conv_ln_g, conv_ln_b, b_forget, w_conv_out, w_ret_out,
           w_fox_out, w_gate, b_gate, w_out, ln1_g, ln1_b, w_router, b_router, w1, w3, w2, ln2_g, ln2_b):
    B, S, D = x.shape
    L = w_ada.shape[0]
    T = B * S
    alpha = (2 * L) ** 0.25

    ada = _ada(c, w_ada, b_ada).reshape(L, B, 6, D)
    cos_t, sin_t = _rotary_tables(S)
    tm_comb = min(TM_COMB, S)

    wr = jnp.zeros((D, LANES), F32).at[:, :N_EXPERTS].set(w_router)
    wr_hi = wr.astype(BF16)
    wr_lo = (wr - wr_hi.astype(F32)).astype(BF16)
    wr2 = jnp.stack([wr_hi, wr_lo])
    br_b = jnp.broadcast_to(b_router.astype(F32)[:, None], (N_EXPERTS, min(TM_PROJ, S)))

    for l in range(L):
        w_in_b = jnp.zeros((D, IN_COLS_PAD), BF16).at[:, :w_in.shape[2]].set(w_in[l].astype(BF16))
        bf_pad = jnp.zeros((1, LANES), F32).at[0, :FOX_HEADS].set(b_forget[l])
        h, rq, rk, rv, rg, fq, fk, fv = _inproj(x, ada[l], w_in_b, cos_t, sin_t, bf_pad)
        conv_act = _conv(h, conv_w[l], conv_b[l], conv_ln_g[l], conv_ln_b[l])
        ret_act = _retention(rq, rk, rv, rg)
        fox_act = _fox(fq, fk, fv)
        x1, u2, ridx, rw = _merge(
            x.reshape(T, D), conv_act.reshape(T, 512), ret_act.reshape(T, 512), fox_act.reshape(T, 512), ada[l],
            w_gate[l].astype(BF16), b_gate[l].reshape(1, -1), w_conv_out[l].astype(BF16), w_ret_out[l].astype(BF16),
            w_fox_out[l].astype(BF16), w_out[l].astype(BF16), ln1_g[l].reshape(1, D), ln1_b[l].reshape(1, D),
            wr2, br_b, S, alpha)
        dest, buf_tok, block_e, nvalid = _plan(ridx, TM_MOE)
        y_sorted = _moe(u2, buf_tok, block_e, nvalid, w1[l].astype(BF16), w3[l].astype(BF16), w2[l].astype(BF16))
        dest3 = dest.reshape(TOP_K, T // tm_comb, tm_comb).transpose(1, 0, 2)
        x = _combine(dest3, y_sorted, x1, rw.T, ada[l], ln2_g[l].reshape(1, D), ln2_b[l].reshape(1, D), S,
                     alpha).reshape(B, S, D)
    return x
```

```python
import functools

import jax
import jax.numpy as jnp
import numpy as np
from jax import lax
from jax.experimental import pallas as pl
from jax.experimental.pallas import tpu as pltpu

F32 = jnp.float32
BF16 = jnp.bfloat16

CHUNK = 64
CONV_CH = 512
CONV_K = 31
RET_HEADS = 4
RET_DK = 64
RET_DV = 128
RET_THETA = 10000.0
FOX_HEADS = 8
FOX_DH = 64
N_EXPERTS = 16
N_GROUPS = 4
EXPERTS_PER_GROUP = 4
TOP_K = 2
LN_EPS = 1e-5
LOG2E = 1.4426950408889634

A_VAL = 0
A_GATE = 512
R_Q = 1024
R_K = 1280
R_V = 1536
R_G = 2048
F_Q = 2560
F_K = 3072
F_V = 3584
F_F = 4096
IN_COLS_PAD = 4224

LANES = 128
CONV_HALO = 32

TM_PROJ = 512
TS_CONV = 512
TS_RET = 256
TQ_FOX = 512
TM_MOE = 512
FF_CHUNK = 512
TM_COMB = 256

VMEM_LIMIT = 56 * 1024 * 1024


def _cparams(sem, vmem=VMEM_LIMIT):
    return pltpu.CompilerParams(dimension_semantics=sem, vmem_limit_bytes=vmem)


def _ln(x):
    mu = jnp.mean(x, axis=-1, keepdims=True)
    xc = x - mu
    var = jnp.mean(xc * xc, axis=-1, keepdims=True)
    return xc * lax.rsqrt(var + LN_EPS)


def _sigmoid(x):
    return 1.0 / (1.0 + jnp.exp(-x))


def _silu(x):
    return x * _sigmoid(x)


def _split2(x):
    hi = x.astype(BF16)
    lo = (x - hi.astype(F32)).astype(BF16)
    return hi, lo


def _split3(x):
    c1 = x.astype(BF16).astype(F32)
    r = x - c1
    c2 = r.astype(BF16).astype(F32)
    c3 = (r - c2).astype(BF16).astype(F32)
    return c1, c2, c3


def _dot(a, b):
    return jnp.dot(a, b, preferred_element_type=F32)


def _dot_nt(a, b):
    return lax.dot_general(a, b, (((1,), (1,)), ((), ())), preferred_element_type=F32)


def _dot_tn(a, b):
    return lax.dot_general(a, b, (((0,), (0,)), ((), ())), preferred_element_type=F32)


def _ada_kernel(c_ref, w_ref, b_ref, o_ref):
    sc = _silu(c_ref[...])
    a_hi, a_lo = _split2(sc)
    w = w_ref[0]
    w_hi, w_lo = _split2(w)
    o_ref[0] = _dot(a_hi, w_hi) + _dot(a_lo, w_hi) + _dot(a_hi, w_lo) + b_ref[0]


def _ada(c, w_ada, b_ada):
    L, D, N = w_ada.shape
    B = c.shape[0]
    tn = 1536
    return pl.pallas_call(
        _ada_kernel,
        grid=(L, N // tn),
        in_specs=[
            pl.BlockSpec((B, D), lambda l, j: (0, 0)),
            pl.BlockSpec((1, D, tn), lambda l, j: (l, 0, j)),
            pl.BlockSpec((1, 1, tn), lambda l, j: (l, 0, j)),
        ],
        out_specs=pl.BlockSpec((1, B, tn), lambda l, j: (l, 0, j)),
        out_shape=jax.ShapeDtypeStruct((L, B, N), F32),
        compiler_params=_cparams(("arbitrary", "arbitrary")),
        name="ada",
    )(c, w_ada, b_ada.reshape(L, 1, N))


def _rotary(z, cos, sin_signed, lane):
    partner = jnp.where((lane & 63) < 32, pltpu.roll(z, 96, axis=1), pltpu.roll(z, 32, axis=1))
    return z * cos + partner * sin_signed


def _inproj_kernel(x_ref, ada_ref, w_ref, cos_ref, sin_ref, bf_ref,
                   h_ref, rq_ref, rk_ref, rv_ref, rg_ref, fq_ref, fk_ref, fv_ref, carry_ref):
    s = pl.program_id(1)
    tm = x_ref.shape[1]

    @pl.when(s == 0)
    def _():
        carry_ref[...] = jnp.zeros_like(carry_ref)

    x = x_ref[0]
    sh1 = ada_ref[0, 0:1, :]
    sc1 = ada_ref[0, 1:2, :]
    u = (_ln(x) * (1.0 + sc1) + sh1).astype(BF16)

    def proj(c0, c1):
        return _dot(u, w_ref[:, c0:c1])

    h_ref[0] = (proj(A_VAL, A_GATE) * _sigmoid(proj(A_GATE, R_Q))).astype(BF16)

    lane = lax.broadcasted_iota(jnp.int32, (tm, LANES), 1)
    cos = cos_ref[...]
    sin = sin_ref[...]
    for p in range(2):
        c0 = R_Q + p * LANES
        rq_ref[0, :, p * LANES:(p + 1) * LANES] = _rotary(proj(c0, c0 + LANES), cos, sin, lane).astype(BF16)
        c0 = R_K + p * LANES
        rk = _rotary(proj(c0, c0 + LANES), cos, sin, lane) * (RET_DK ** -0.5)
        rk_ref[0, :, p * LANES:(p + 1) * LANES] = rk.astype(BF16)
    rv_ref[0] = proj(R_V, R_G).astype(BF16)
    rg_ref[0] = _silu(proj(R_G, F_Q)).astype(BF16)
    fv_ref[0] = proj(F_V, F_F).astype(BF16)

    f_logit = proj(F_F, IN_COLS_PAD) + bf_ref[...]
    log_f = jnp.minimum(f_logit, 0.0) - jnp.log(1.0 + jnp.exp(-jnp.abs(f_logit)))
    row = lax.broadcasted_iota(jnp.int32, (tm, LANES), 0)
    y = log_f
    sh = 1
    while sh < tm:
        y = y + jnp.where(row >= sh, pltpu.roll(y, sh, axis=0), 0.0)
        sh *= 2
    cum = y + carry_ref[0:1, :]
    carry_ref[0:1, :] = cum[tm - 1:tm, :]
    c1, c2, c3 = _split3(cum * LOG2E)

    one_q = jnp.where((lane >= 67) & (lane < 70), 1.0, 0.0)
    one_k = jnp.where((lane >= 64) & (lane < 67), 1.0, 0.0)
    for p in range(FOX_HEADS // 2):
        zq = proj(F_Q + p * LANES, F_Q + (p + 1) * LANES) * (FOX_DH ** -0.5 * LOG2E)
        zk = proj(F_K + p * LANES, F_K + (p + 1) * LANES)
        zq_sw = pltpu.roll(zq, 64, axis=1)
        zk_sw = pltpu.roll(zk, 64, axis=1)
        for hh in range(2):
            h = 2 * p + hh
            a1 = c1[:, h:h + 1]
            a2 = c2[:, h:h + 1]
            a3 = c3[:, h:h + 1]
            ext_q = jnp.where(lane == 64, a1, jnp.where(lane == 65, a2, jnp.where(lane == 66, a3, one_q)))
            ext_k = jnp.where(lane == 67, -a1, jnp.where(lane == 68, -a2, jnp.where(lane == 69, -a3, one_k)))
            fq_ref[0, h] = jnp.where(lane < 64, zq if hh == 0 else zq_sw, ext_q).astype(BF16)
            fk_ref[0, h] = jnp.where(lane < 64, zk if hh == 0 else zk_sw, ext_k).astype(BF16)


def _inproj(x, ada_l, w_in_b, cos_t, sin_t, bf_pad):
    B, S, D = x.shape
    tm = min(TM_PROJ, S)
    bsd = lambda n, dt: jax.ShapeDtypeStruct((B, S, n), dt)
    row_spec = lambda n: pl.BlockSpec((1, tm, n), lambda b, s: (b, s, 0))
    aug_spec = pl.BlockSpec((1, FOX_HEADS, tm, LANES), lambda b, s: (b, 0, s, 0))
    aug_shape = jax.ShapeDtypeStruct((B, FOX_HEADS, S, LANES), BF16)
    return pl.pallas_call(
        _inproj_kernel,
        grid=(B, S // tm),
        in_specs=[
            row_spec(D),
            pl.BlockSpec((1, 6, D), lambda b, s: (b, 0, 0)),
            pl.BlockSpec((D, IN_COLS_PAD), lambda b, s: (0, 0)),
            pl.BlockSpec((tm, LANES), lambda b, s: (s, 0)),
            pl.BlockSpec((tm, LANES), lambda b, s: (s, 0)),
            pl.BlockSpec((1, LANES), lambda b, s: (0, 0)),
        ],
        out_specs=[row_spec(CONV_CH), row_spec(256), row_spec(256), row_spec(512), row_spec(512),
                   aug_spec, aug_spec, row_spec(512)],
        out_shape=[bsd(CONV_CH, BF16), bsd(256, BF16), bsd(256, BF16), bsd(512, BF16), bsd(512, BF16),
                   aug_shape, aug_shape, bsd(512, BF16)],
        scratch_shapes=[pltpu.VMEM((8, LANES), F32)],
        compiler_params=_cparams(("arbitrary", "arbitrary")),
        name="inproj",
    )(x, ada_l, w_in_b, cos_t, sin_t, bf_pad)


def _conv_kernel(h_ref, w_ref, b_ref, g_ref, beta_ref, o_ref, hbuf):
    s = pl.program_id(1)
    ts = h_ref.shape[1]

    @pl.when(s == 0)
    def _():
        hbuf[0:CONV_HALO, :] = jnp.zeros((CONV_HALO, CONV_CH), F32)

    hbuf[CONV_HALO:CONV_HALO + ts, :] = h_ref[0].astype(F32)
    acc = jnp.zeros((ts, CONV_CH), F32) + b_ref[...]
    off = CONV_HALO - (CONV_K - 1)
    for k in range(CONV_K):
        acc = acc + w_ref[k:k + 1, :] * hbuf[off + k:off + k + ts, :]
    y = _ln(acc) * g_ref[...] + beta_ref[...]
    o_ref[0] = _silu(y).astype(BF16)
    hbuf[0:CONV_HALO, :] = hbuf[ts:ts + CONV_HALO, :]


def _conv(h, conv_w, conv_b, g, beta):
    B, S, C = h.shape
    ts = min(TS_CONV, S)
    w_pad = jnp.zeros((32, C), F32).at[:CONV_K].set(conv_w)
    vec = lambda: pl.BlockSpec((1, C), lambda b, s: (0, 0))
    return pl.pallas_call(
        _conv_kernel,
        grid=(B, S // ts),
        in_specs=[pl.BlockSpec((1, ts, C), lambda b, s: (b, s, 0)),
                  pl.BlockSpec((32, C), lambda b, s: (0, 0)), vec(), vec(), vec()],
        out_specs=pl.BlockSpec((1, ts, C), lambda b, s: (b, s, 0)),
        out_shape=jax.ShapeDtypeStruct((B, S, C), BF16),
        scratch_shapes=[pltpu.VMEM((ts + CONV_HALO, C), F32)],
        compiler_params=_cparams(("arbitrary", "arbitrary")),
        name="conv",
    )(h, w_pad, conv_b.reshape(1, C), g.reshape(1, C), beta.reshape(1, C))


def _ret_kernel(q_ref, k_ref, v_ref, g_ref, mask_ref, qdec_ref, kdec_ref, sdec_ref, o_ref, state):
    s = pl.program_id(1)
    ts = q_ref.shape[1]

    @pl.when(s == 0)
    def _():
        state[...] = jnp.zeros_like(state)

    lane = lax.broadcasted_iota(jnp.int32, (ts, LANES), 1)
    for p in range(RET_HEADS // 2):
        q = q_ref[0, :, p * LANES:(p + 1) * LANES]
        kp = k_ref[0, :, p * LANES:(p + 1) * LANES]
        for hh in range(2):
            h = 2 * p + hh
            own = (lane < 64) if hh == 0 else (lane >= 64)
            k = jnp.where(own, kp, jnp.zeros_like(kp))
            v = v_ref[0, :, h * RET_DV:(h + 1) * RET_DV]
            sc = _dot_nt(q, k) * mask_ref[h]
            st = state[h]
            o = _dot(sc.astype(BF16), v) + _dot(q, st.astype(BF16)) * qdec_ref[h]
            kd = (k.astype(F32) * kdec_ref[h]).astype(BF16)
            state[h] = st * sdec_ref[h] + _dot_tn(kd, v)
            ry = _ln(o)
            o_ref[0, :, h * RET_DV:(h + 1) * RET_DV] = (g_ref[0, :, h * RET_DV:(h + 1) * RET_DV].astype(F32) * ry).astype(BF16)


def _ret_tables(ts):
    log_gamma = jnp.log1p(-jnp.exp2(-5.0 - jnp.arange(RET_HEADS, dtype=F32)))
    t = jnp.arange(ts)
    dt = (t[:, None] - t[None, :]).astype(F32)
    same = (t[:, None] // CHUNK) == (t[None, :] // CHUNK)
    earlier = (t[None, :] // CHUNK) < (t[:, None] // CHUNK)
    expo = jnp.where(same, jnp.abs(dt), dt)
    w = jnp.exp(log_gamma[:, None, None] * expo[None])
    mask = jnp.where((same | earlier)[None], w, 0.0)
    tf = t.astype(F32)
    qdec = jnp.broadcast_to(jnp.exp(log_gamma[:, None] * (tf[None, :] + 1.0))[:, :, None], (RET_HEADS, ts, RET_DV))
    kdec = jnp.broadcast_to(jnp.exp(log_gamma[:, None] * (ts - 1.0 - tf[None, :]))[:, :, None], (RET_HEADS, ts, LANES))
    sdec = jnp.broadcast_to(jnp.exp(log_gamma * ts)[:, None, None], (RET_HEADS, LANES, RET_DV))
    return mask.astype(F32), qdec.astype(F32), kdec.astype(F32), sdec.astype(F32)


def _retention(rq, rk, rv, rg):
    B, S, _ = rq.shape
    ts = min(TS_RET, S)
    mask, qdec, kdec, sdec = _ret_tables(ts)
    row = lambda n: pl.BlockSpec((1, ts, n), lambda b, s: (b, s, 0))
    full = lambda a: pl.BlockSpec(a.shape, lambda b, s: (0, 0, 0))
    return pl.pallas_call(
        _ret_kernel,
        grid=(B, S // ts),
        in_specs=[row(256), row(256), row(512), row(512), full(mask), full(qdec), full(kdec), full(sdec)],
        out_specs=row(512),
        out_shape=jax.ShapeDtypeStruct((B, S, 512), BF16),
        scratch_shapes=[pltpu.VMEM((RET_HEADS, LANES, RET_DV), F32)],
        compiler_params=_cparams(("arbitrary", "arbitrary")),
        name="retention",
    )(rq, rk, rv, rg, mask, qdec, kdec, sdec)


def _fox_kernel(q_ref, k_ref, v_ref, o_ref, vaug):
    qi = pl.program_id(2)
    tq = q_ref.shape[2]
    tk = tq
    S = v_ref.shape[1]

    @pl.when(qi == 0)
    def _():
        vaug[:, 0:LANES] = v_ref[0]
        vaug[:, LANES:2 * LANES] = jnp.ones((S, LANES), BF16)

    qs = (q_ref[0, 0], q_ref[0, 1])

    def step(j, carry, masked):
        off = pl.multiple_of(j * tk, tk)
        v = vaug[pl.ds(off, tk), :]
        new = []
        for hh in range(2):
            m, acc = carry[hh]
            k = k_ref[0, hh, pl.ds(off, tk), :]
            s = _dot_nt(qs[hh], k)
            if masked:
                r = lax.broadcasted_iota(jnp.int32, (tq, tk), 0)
                c = lax.broadcasted_iota(jnp.int32, (tq, tk), 1)
                s = jnp.where(r >= c, s, -1e30)
            m_new = jnp.maximum(m, jnp.max(s, axis=-1, keepdims=True))
            p = jnp.exp2(s - m_new)
            alpha = jnp.exp2(m - m_new)
            acc = alpha * acc + _dot(p.astype(BF16), v)
            new.append((m_new, acc))
        return tuple(new)

    init = tuple((jnp.full((tq, 1), -1e30, F32), jnp.zeros((tq, 2 * LANES), F32)) for _ in range(2))
    carry = lax.fori_loop(0, qi, functools.partial(step, masked=False), init)
    carry = step(qi, carry, True)
    outs = [acc[:, 0:LANES] / acc[:, LANES:LANES + 1] for _, acc in carry]
    lane = lax.broadcasted_iota(jnp.int32, (tq, LANES), 1)
    o_ref[0] = jnp.where(lane < 64, outs[0], outs[1]).astype(BF16)


def _fox(fq, fk, fv):
    B, H, S, _ = fq.shape
    tq = min(TQ_FOX, S)
    return pl.pallas_call(
        _fox_kernel,
        grid=(B, H // 2, S // tq),
        in_specs=[
            pl.BlockSpec((1, 2, tq, LANES), lambda b, p, i: (b, p, i, 0)),
            pl.BlockSpec((1, 2, S, LANES), lambda b, p, i: (b, p, 0, 0)),
            pl.BlockSpec((1, S, LANES), lambda b, p, i: (b, 0, p)),
        ],
        out_specs=pl.BlockSpec((1, tq, LANES), lambda b, p, i: (b, i, p)),
        out_shape=jax.ShapeDtypeStruct((B, S, H * FOX_DH), BF16),
        scratch_shapes=[pltpu.VMEM((S, 2 * LANES), BF16)],
        compiler_params=_cparams(("arbitrary", "arbitrary", "arbitrary")),
        name="fox",
    )(fq, fk, fv)


def _top2_of4(a, b, c, d):
    m1, n1 = jnp.maximum(a, b), jnp.minimum(a, b)
    m2, n2 = jnp.maximum(c, d), jnp.minimum(c, d)
    return jnp.maximum(m1, m2) + jnp.maximum(jnp.minimum(m1, m2), jnp.maximum(n1, n2))


def _route_rows(score, sel):
    gs = [_top2_of4(*sel[4 * g:4 * g + 4]) for g in range(N_GROUPS)]
    best = gs[0]
    bi = jnp.zeros_like(best, dtype=jnp.int32)
    for g in range(1, N_GROUPS):
        better = gs[g] > best
        bi = jnp.where(better, g, bi)
        best = jnp.where(better, gs[g], best)

    def pick(rows, j):
        out = rows[j]
        for g in range(1, N_GROUPS):
            out = jnp.where(bi == g, rows[4 * g + j], out)
        return out

    m = [pick(sel, j) for j in range(4)]
    sc = [pick(score, j) for j in range(4)]

    def argmax4(vals):
        bv, bj = vals[0], jnp.zeros_like(bi)
        for j in range(1, 4):
            better = vals[j] > bv
            bj = jnp.where(better, j, bj)
            bv = jnp.where(better, vals[j], bv)
        return bj

    i1 = argmax4(m)
    i2 = argmax4([jnp.where(i1 == j, -jnp.inf, m[j]) for j in range(4)])

    def take(vals, idx):
        out = vals[0]
        for j in range(1, 4):
            out = jnp.where(idx == j, vals[j], out)
        return out

    w1, w2 = take(sc, i1), take(sc, i2)
    tot = w1 + w2
    return 4 * bi + i1, 4 * bi + i2, w1 / tot, w2 / tot


def _merge_kernel(x_ref, ca_ref, ra_ref, fa_ref, ada_ref, wg_ref, bg_ref, wco_ref, wro_ref, wfo_ref, wout_ref,
                  g1_ref, b1_ref, wr_ref, br_ref, x1_ref, u2_ref, ridx_ref, rw_ref, *, alpha):
    D = x_ref.shape[1]
    x = x_ref[...]
    sh1 = ada_ref[0, 0:1, :]
    sc1 = ada_ref[0, 1:2, :]
    g1 = ada_ref[0, 2:3, :]
    sh2 = ada_ref[0, 3:4, :]
    sc2 = ada_ref[0, 4:5, :]
    u = (_ln(x) * (1.0 + sc1) + sh1).astype(BF16)
    m = None
    for i, (a_ref, w_ref) in enumerate(((ca_ref, wco_ref), (ra_ref, wro_ref), (fa_ref, wfo_ref))):
        gate = _sigmoid(_dot(u, wg_ref[:, i * D:(i + 1) * D]) + bg_ref[:, i * D:(i + 1) * D])
        t = gate * _dot(a_ref[...], w_ref[...])
        m = t if m is None else m + t
    h = _dot(m.astype(BF16), wout_ref[...])
    x1 = _ln(alpha * x + g1 * h) * g1_ref[...] + b1_ref[...]
    x1_ref[...] = x1
    u2 = _ln(x1) * (1.0 + sc2) + sh2
    u2_ref[...] = u2
    u_hi, u_lo = _split2(u2)
    logits = _dot(u_hi, wr_ref[0]) + _dot(u_lo, wr_ref[0]) + _dot(u_hi, wr_ref[1])
    lt = jnp.transpose(logits)
    score = [_sigmoid(lt[e:e + 1, :]) for e in range(N_EXPERTS)]
    sel = [score[e] + br_ref[e:e + 1, :] for e in range(N_EXPERTS)]
    e1, e2, w1, w2 = _route_rows(score, sel)
    ridx_ref[0:1, :] = e1
    ridx_ref[1:2, :] = e2
    rw_ref[0:1, :] = w1
    rw_ref[1:2, :] = w2


def _merge(x2, ca, ra, fa, ada_l, wg, bg, wco, wro, wfo, wout, g1, b1, wr2, br_b, S, alpha):
    T, D = x2.shape
    tm = min(TM_PROJ, S)
    spb = S // tm
    row = lambda n: pl.BlockSpec((tm, n), lambda i: (i, 0))
    const = lambda a: pl.BlockSpec(a.shape, lambda i: (0,) * a.ndim)
    return pl.pallas_call(
        functools.partial(_merge_kernel, alpha=alpha),
        grid=(T // tm,),
        in_specs=[row(D), row(512), row(512), row(512),
                  pl.BlockSpec((1, 6, D), lambda i: (i // spb, 0, 0)),
                  const(wg), const(bg), const(wco), const(wro), const(wfo), const(wout), const(g1), const(b1),
                  const(wr2), const(br_b)],
        out_specs=[row(D), row(D), pl.BlockSpec((2, tm), lambda i: (0, i)), pl.BlockSpec((2, tm), lambda i: (0, i))],
        out_shape=[jax.ShapeDtypeStruct((T, D), F32), jax.ShapeDtypeStruct((T, D), F32),
                   jax.ShapeDtypeStruct((2, T), jnp.int32), jax.ShapeDtypeStruct((2, T), F32)],
        compiler_params=_cparams(("arbitrary",)),
        name="merge",
    )(x2, ca, ra, fa, ada_l, wg, bg, wco, wro, wfo, wout, g1, b1, wr2, br_b)


def _moe_kernel(be_ref, tok_cur_ref, tok_nxt_ref, sd_prev_ref, sd_cur_ref, x_hbm, w1_ref, w3_ref, w2_ref, y_hbm,
                xbuf, ybuf, xb, gsem, ssem):
    i = pl.program_id(0)
    nb = pl.num_programs(0)
    tm = xbuf.shape[1]
    slot = i % 2
    other = 1 - slot

    def gather_row(tok_ref, r, dst):
        pltpu.make_async_copy(x_hbm.at[pl.ds(tok_ref[0, r], 1), :], xbuf.at[dst, pl.ds(r, 1), :], gsem.at[dst]).start()

    def scatter_row(sd_ref, r, src):
        pltpu.make_async_copy(ybuf.at[src, pl.ds(r, 1), :], y_hbm.at[pl.ds(sd_ref[0, r], 1), :], ssem.at[src]).start()

    def wait_rows(sem, buf):
        pltpu.make_async_copy(x_hbm.at[pl.ds(0, tm), :], buf, sem).wait()

    @pl.when(i == 0)
    def _():
        ybuf[...] = jnp.zeros_like(ybuf)

        def body(r, c):
            gather_row(tok_cur_ref, r, 0)
            return c
        lax.fori_loop(0, tm, body, 0)

    wait_rows(gsem.at[slot], xbuf.at[slot])
    xb[...] = xbuf[slot].astype(BF16)
    x = xb[...]
    n_chunk = w1_ref.shape[2] // FF_CHUNK
    bounds = [tm * c // (n_chunk - 1) for c in range(n_chunk)] + [tm]
    acc = None
    for c in range(n_chunk):
        for r in range(bounds[c], bounds[c + 1]):
            gather_row(tok_nxt_ref, r, other)
            scatter_row(sd_prev_ref, r, other)
        c0 = c * FF_CHUNK
        h1 = _dot(x, w1_ref[0, :, c0:c0 + FF_CHUNK])
        h3 = _dot(x, w3_ref[0, :, c0:c0 + FF_CHUNK])
        h = (_silu(h1) * h3).astype(BF16)
        t = _dot(h, w2_ref[0, c0:c0 + FF_CHUNK, :])
        acc = t if acc is None else acc + t
        pl.delay(1)

    @pl.when(i >= 1)
    def _():
        wait_rows(ssem.at[slot], ybuf.at[slot])

    ybuf[slot] = acc

    @pl.when(i == nb - 1)
    def _():
        wait_rows(ssem.at[other], ybuf.at[other])

        def body(r, c):
            scatter_row(sd_cur_ref, r, slot)
            return c
        lax.fori_loop(0, tm, body, 0)
        wait_rows(ssem.at[slot], ybuf.at[slot])
        wait_rows(gsem.at[other], xbuf.at[other])


def _moe(u2, buf_tok, sdest, block_e, w1b, w3b, w2b, n_out_rows):
    T, D = u2.shape
    E, _, FF = w1b.shape
    nb = buf_tok.shape[0]
    tm = buf_tok.shape[2]
    idx_spec = lambda f: pl.BlockSpec((None, 1, tm), f, memory_space=pltpu.SMEM)
    grid_spec = pltpu.PrefetchScalarGridSpec(
        num_scalar_prefetch=1,
        grid=(nb,),
        in_specs=[
            idx_spec(lambda i, be: (i, 0, 0)),
            idx_spec(lambda i, be: (jnp.minimum(i + 1, nb - 1), 0, 0)),
            idx_spec(lambda i, be: (i, 0, 0)),
            idx_spec(lambda i, be: (i + 1, 0, 0)),
            pl.BlockSpec(memory_space=pl.ANY),
            pl.BlockSpec((1, D, FF), lambda i, be: (be[i], 0, 0)),
            pl.BlockSpec((1, D, FF), lambda i, be: (be[i], 0, 0)),
            pl.BlockSpec((1, FF, D), lambda i, be: (be[i], 0, 0)),
        ],
        out_specs=pl.BlockSpec(memory_space=pl.ANY),
        scratch_shapes=[pltpu.VMEM((2, tm, D), F32), pltpu.VMEM((2, tm, D), F32), pltpu.VMEM((tm, D), BF16),
                        pltpu.SemaphoreType.DMA((2,)), pltpu.SemaphoreType.DMA((2,))],
    )
    return pl.pallas_call(
        _moe_kernel,
        grid_spec=grid_spec,
        out_shape=jax.ShapeDtypeStruct((n_out_rows, D), F32),
        compiler_params=_cparams(("arbitrary",)),
        name="moe",
    )(block_e, buf_tok, buf_tok, sdest, sdest, u2, w1b, w3b, w2b)


def _combine_kernel(y0_ref, y1_ref, x1_ref, w_ref, ada_ref, g_ref, b_ref, o_ref, *, alpha):
    g2 = ada_ref[0, 5:6, :]
    w = w_ref[...]
    h2 = w[:, 0:1] * y0_ref[...] + w[:, 1:2] * y1_ref[...]
    o_ref[...] = _ln(alpha * x1_ref[...] + g2 * h2) * g_ref[...] + b_ref[...]


def _combine(y_tok, x1, wts, ada_l, g2v, b2v, S, alpha):
    T, D = x1.shape
    tm = min(TM_COMB, S)
    spb = S // tm
    nb = T // tm
    const = lambda a: pl.BlockSpec(a.shape, lambda i: (0,) * a.ndim)
    return pl.pallas_call(
        functools.partial(_combine_kernel, alpha=alpha),
        grid=(nb,),
        in_specs=[
            pl.BlockSpec((tm, D), lambda i: (i, 0)),
            pl.BlockSpec((tm, D), lambda i: (i + nb, 0)),
            pl.BlockSpec((tm, D), lambda i: (i, 0)),
            pl.BlockSpec((tm, TOP_K), lambda i: (i, 0)),
            pl.BlockSpec((1, 6, D), lambda i: (i // spb, 0, 0)),
            const(g2v), const(b2v),
        ],
        out_specs=pl.BlockSpec((tm, D), lambda i: (i, 0)),
        out_shape=jax.ShapeDtypeStruct((T, D), F32),
        compiler_params=_cparams(("arbitrary",)),
        name="combine",
    )(y_tok, y_tok, x1, wts, ada_l, g2v, b2v)


def _plan(ridx, tm):
    _, T = ridx.shape
    A = TOP_K * T
    nb = A // tm + N_EXPERTS
    P = nb * tm
    flat_e = ridx.reshape(A)
    experts = jnp.arange(N_EXPERTS, dtype=jnp.int32)
    counts = jnp.sum((flat_e[:, None] == experts[None, :]).astype(jnp.int32), axis=0)
    start = jnp.cumsum(counts) - counts
    padded = (counts + tm - 1) // tm * tm
    pend = jnp.cumsum(padded)
    pstart = pend - padded
    by_expert = jnp.sort(flat_e * A + jnp.arange(A, dtype=jnp.int32)) % A
    block_starts = jnp.arange(nb, dtype=jnp.int32) * tm
    block_e = jnp.minimum(jnp.sum((pend[None, :] <= block_starts[:, None]).astype(jnp.int32), axis=1),
                          N_EXPERTS - 1).astype(jnp.int32)
    p = jnp.arange(P, dtype=jnp.int32)
    e_p = jnp.repeat(block_e, tm)
    j = p - pstart[e_p]
    valid = j < counts[e_p]
    a_p = by_expert[jnp.clip(start[e_p] + j, 0, A - 1)]
    buf_tok = jnp.where(valid, a_p % T, 0).astype(jnp.int32)
    sdest = jnp.where(valid, a_p, A + p - start[e_p] - counts[e_p]).astype(jnp.int32)
    sdest = jnp.concatenate([P + jnp.arange(tm, dtype=jnp.int32), sdest])
    return buf_tok.reshape(nb, 1, tm), sdest.reshape(nb + 1, 1, tm), block_e, P + tm


def _rotary_tables(S):
    half = RET_DK // 2
    inv = RET_THETA ** (-jnp.arange(half, dtype=F32) / half)
    ang = jnp.arange(S, dtype=F32)[:, None] * inv[None, :]
    cos, sin = jnp.cos(ang), jnp.sin(ang)
    cos_t = jnp.tile(jnp.concatenate([cos, cos], axis=1), (1, LANES // RET_DK))
    sin_t = jnp.tile(jnp.concatenate([-sin, sin], axis=1), (1, LANES // RET_DK))
    return cos_t, sin_t


def kernel(x, c, w_ada, b_ada, w_in, conv_w, conv_b, conv_ln_g, conv_ln_b, b_forget, w_conv_out, w_ret_out,
           w_fox_out, w_gate, b_gate, w_out, ln1_g, ln1_b, w_router, b_router, w1, w3, w2, ln2_g, ln2_b):
    B, S, D = x.shape
    L = w_ada.shape[0]
    T = B * S
    alpha = (2 * L) ** 0.25

    ada = _ada(c, w_ada, b_ada).reshape(L, B, 6, D)
    cos_t, sin_t = _rotary_tables(S)

    wr = jnp.zeros((D, LANES), F32).at[:, :N_EXPERTS].set(w_router)
    wr_hi = wr.astype(BF16)
    wr_lo = (wr - wr_hi.astype(F32)).astype(BF16)
    wr2 = jnp.stack([wr_hi, wr_lo])
    br_b = jnp.broadcast_to(b_router.astype(F32)[:, None], (N_EXPERTS, min(TM_PROJ, S)))

    for l in range(L):
        w_in_b = jnp.zeros((D, IN_COLS_PAD), BF16).at[:, :w_in.shape[2]].set(w_in[l].astype(BF16))
        bf_pad = jnp.zeros((1, LANES), F32).at[0, :FOX_HEADS].set(b_forget[l])
        h, rq, rk, rv, rg, fq, fk, fv = _inproj(x, ada[l], w_in_b, cos_t, sin_t, bf_pad)
        conv_act = _conv(h, conv_w[l], conv_b[l], conv_ln_g[l], conv_ln_b[l])
        ret_act = _retention(rq, rk, rv, rg)
        fox_act = _fox(fq, fk, fv)
        x1, u2, ridx, rw = _merge(
            x.reshape(T, D), conv_act.reshape(T, 512), ret_act.reshape(T, 512), fox_act.reshape(T, 512), ada[l],
            w_gate[l].astype(BF16), b_gate[l].reshape(1, -1), w_conv_out[l].astype(BF16), w_ret_out[l].astype(BF16),
            w_fox_out[l].astype(BF16), w_out[l].astype(BF16), ln1_g[l].reshape(1, D), ln1_b[l].reshape(1, D),
            wr2, br_b, S, alpha)
        buf_tok, sdest, block_e, n_out_rows = _plan(ridx, TM_MOE)
        y_tok = _moe(u2, buf_tok, sdest, block_e, w1[l].astype(BF16), w3[l].astype(BF16), w2[l].astype(BF16),
                     n_out_rows)
        x = _combine(y_tok, x1, rw.T, ada[l], ln2_g[l].reshape(1, D), ln2_b[l].reshape(1, D), S,
                     alpha).reshape(B, S, D)
    return x
```

```python
import functools

import jax
import jax.numpy as jnp
import numpy as np
from jax import lax
from jax.experimental import pallas as pl
from jax.experimental.pallas import tpu as pltpu

F32 = jnp.float32
BF16 = jnp.bfloat16

CHUNK = 64
CONV_CH = 512
CONV_K = 31
RET_HEADS = 4
RET_DK = 64
RET_DV = 128
RET_THETA = 10000.0
FOX_HEADS = 8
FOX_DH = 64
N_EXPERTS = 16
N_GROUPS = 4
EXPERTS_PER_GROUP = 4
TOP_K = 2
LN_EPS = 1e-5
LOG2E = 1.4426950408889634

A_VAL = 0
A_GATE = 512
R_Q = 1024
R_K = 1280
R_V = 1536
R_G = 2048
F_Q = 2560
F_K = 3072
F_V = 3584
F_F = 4096
IN_COLS_PAD = 4224

LANES = 128
SUBLANES = 8
CONV_HALO = 32

TM_PROJ = 512
TS_CONV = 512
TS_RET = 256
TQ_FOX = 512
FOX_HEADS_PER_STEP = 4
TM_MOE = 512
FF_CHUNK = 512
TM_COMB = 256
MERGE_SUB = 256
PROJ_SUB = 256

VMEM_LIMIT = 56 * 1024 * 1024


def _cparams(sem, vmem=VMEM_LIMIT):
    return pltpu.CompilerParams(dimension_semantics=sem, vmem_limit_bytes=vmem)


def _ln(x):
    mu = jnp.mean(x, axis=-1, keepdims=True)
    xc = x - mu
    var = jnp.mean(xc * xc, axis=-1, keepdims=True)
    return xc * lax.rsqrt(var + LN_EPS)


def _sigmoid(x):
    return 1.0 / (1.0 + jnp.exp(-x))


def _silu(x):
    return x * _sigmoid(x)


def _split2(x):
    hi = x.astype(BF16)
    lo = (x - hi.astype(F32)).astype(BF16)
    return hi, lo


def _split3(x):
    c1 = x.astype(BF16).astype(F32)
    r = x - c1
    c2 = r.astype(BF16).astype(F32)
    c3 = (r - c2).astype(BF16).astype(F32)
    return c1, c2, c3


def _dot(a, b):
    return jnp.dot(a, b, preferred_element_type=F32)


def _dot_nt(a, b):
    return lax.dot_general(a, b, (((1,), (1,)), ((), ())), preferred_element_type=F32)


def _dot_tn(a, b):
    return lax.dot_general(a, b, (((0,), (0,)), ((), ())), preferred_element_type=F32)


def _ada_kernel(c_ref, w_ref, b_ref, o_ref):
    sc = _silu(c_ref[...])
    a_hi, a_lo = _split2(sc)
    w = w_ref[0]
    w_hi, w_lo = _split2(w)
    o_ref[0] = _dot(a_hi, w_hi) + _dot(a_lo, w_hi) + _dot(a_hi, w_lo) + b_ref[0]


def _ada(c, w_ada, b_ada):
    L, D, N = w_ada.shape
    B = c.shape[0]
    tn = 1536
    return pl.pallas_call(
        _ada_kernel,
        grid=(L, N // tn),
        in_specs=[
            pl.BlockSpec((B, D), lambda l, j: (0, 0)),
            pl.BlockSpec((1, D, tn), lambda l, j: (l, 0, j)),
            pl.BlockSpec((1, 1, tn), lambda l, j: (l, 0, j)),
        ],
        out_specs=pl.BlockSpec((1, B, tn), lambda l, j: (l, 0, j)),
        out_shape=jax.ShapeDtypeStruct((L, B, N), F32),
        compiler_params=_cparams(("arbitrary", "arbitrary")),
        name="ada",
    )(c, w_ada, b_ada.reshape(L, 1, N))


def _rotary(z, cos, sin_signed, lane):
    partner = jnp.where((lane & 63) < 32, pltpu.roll(z, 96, axis=1), pltpu.roll(z, 32, axis=1))
    return z * cos + partner * sin_signed


def _inproj_kernel(x_ref, ada_ref, w_ref, cos_ref, sin_ref, bf_ref,
                   h_ref, rq_ref, rk_ref, rv_ref, rg_ref, fq_ref, fk_ref, fv_ref, carry_ref):
    s = pl.program_id(1)
    tm = x_ref.shape[1]

    @pl.when(s == 0)
    def _():
        carry_ref[...] = jnp.zeros_like(carry_ref)

    sh1 = ada_ref[0, 0:1, :]
    sc1 = ada_ref[0, 1:2, :]
    sub = min(PROJ_SUB, tm)
    lane = lax.broadcasted_iota(jnp.int32, (sub, LANES), 1)
    row = lax.broadcasted_iota(jnp.int32, (sub, LANES), 0)
    one_q = jnp.where((lane >= 67) & (lane < 70), 1.0, 0.0)
    one_k = jnp.where((lane >= 64) & (lane < 67), 1.0, 0.0)
    for r0 in range(0, tm, sub):
        rows = slice(r0, r0 + sub)
        u = (_ln(x_ref[0, rows, :]) * (1.0 + sc1) + sh1).astype(BF16)

        def proj(c0, c1):
            return _dot(u, w_ref[:, c0:c1])

        h_ref[0, rows, :] = (proj(A_VAL, A_GATE) * _sigmoid(proj(A_GATE, R_Q))).astype(BF16)

        cos = cos_ref[rows, :]
        sin = sin_ref[rows, :]
        z_rq = proj(R_Q, R_K)
        z_rk = proj(R_K, R_V)
        for p in range(2):
            cols = slice(p * LANES, (p + 1) * LANES)
            rq_ref[0, rows, cols] = _rotary(z_rq[:, cols], cos, sin, lane).astype(BF16)
            rk = _rotary(z_rk[:, cols], cos, sin, lane) * (RET_DK ** -0.5)
            rk_ref[0, rows, cols] = rk.astype(BF16)
        rv_ref[0, rows, :] = proj(R_V, R_G).astype(BF16)
        rg_ref[0, rows, :] = _silu(proj(R_G, F_Q)).astype(BF16)
        z_fv = proj(F_V, IN_COLS_PAD)
        fv_ref[0, rows, :] = z_fv[:, 0:F_F - F_V].astype(BF16)

        f_logit = z_fv[:, F_F - F_V:] + bf_ref[...]
        log_f = jnp.minimum(f_logit, 0.0) - jnp.log(1.0 + jnp.exp(-jnp.abs(f_logit)))
        y = log_f
        sh = 1
        while sh < sub:
            y = y + jnp.where(row >= sh, pltpu.roll(y, sh, axis=0), 0.0)
            sh *= 2
        cum = y + carry_ref[0:1, :]
        carry_ref[0:1, :] = cum[sub - 1:sub, :]
        c1, c2, c3 = _split3(cum * LOG2E)

        z_fq = proj(F_Q, F_K)
        z_fk = proj(F_K, F_V)
        for p in range(FOX_HEADS // 2):
            zq = z_fq[:, p * LANES:(p + 1) * LANES] * (FOX_DH ** -0.5 * LOG2E)
            zk = z_fk[:, p * LANES:(p + 1) * LANES]
            zq_sw = pltpu.roll(zq, 64, axis=1)
            zk_sw = pltpu.roll(zk, 64, axis=1)
            for hh in range(2):
                h = 2 * p + hh
                a1 = c1[:, h:h + 1]
                a2 = c2[:, h:h + 1]
                a3 = c3[:, h:h + 1]
                ext_q = jnp.where(lane == 64, a1, jnp.where(lane == 65, a2, jnp.where(lane == 66, a3, one_q)))
                ext_k = jnp.where(lane == 67, -a1, jnp.where(lane == 68, -a2, jnp.where(lane == 69, -a3, one_k)))
                fq_ref[0, h, rows, :] = jnp.where(lane < 64, zq if hh == 0 else zq_sw, ext_q).astype(BF16)
                fk_ref[0, h, rows, :] = jnp.where(lane < 64, zk if hh == 0 else zk_sw, ext_k).astype(BF16)


def _inproj(x, ada_l, w_in_b, cos_t, sin_t, bf_pad):
    B, S, D = x.shape
    tm = min(TM_PROJ, S)
    bsd = lambda n, dt: jax.ShapeDtypeStruct((B, S, n), dt)
    row_spec = lambda n: pl.BlockSpec((1, tm, n), lambda b, s: (b, s, 0))
    aug_spec = pl.BlockSpec((1, FOX_HEADS, tm, LANES), lambda b, s: (b, 0, s, 0))
    aug_shape = jax.ShapeDtypeStruct((B, FOX_HEADS, S, LANES), BF16)
    return pl.pallas_call(
        _inproj_kernel,
        grid=(B, S // tm),
        in_specs=[
            row_spec(D),
            pl.BlockSpec((1, 6, D), lambda b, s: (b, 0, 0)),
            pl.BlockSpec((D, IN_COLS_PAD), lambda b, s: (0, 0)),
            pl.BlockSpec((tm, LANES), lambda b, s: (s, 0)),
            pl.BlockSpec((tm, LANES), lambda b, s: (s, 0)),
            pl.BlockSpec((1, LANES), lambda b, s: (0, 0)),
        ],
        out_specs=[row_spec(CONV_CH), row_spec(256), row_spec(256), row_spec(512), row_spec(512),
                   aug_spec, aug_spec, row_spec(512)],
        out_shape=[bsd(CONV_CH, BF16), bsd(256, BF16), bsd(256, BF16), bsd(512, BF16), bsd(512, BF16),
                   aug_shape, aug_shape, bsd(512, BF16)],
        scratch_shapes=[pltpu.VMEM((8, LANES), F32)],
        compiler_params=_cparams(("arbitrary", "arbitrary")),
        name="inproj",
    )(x, ada_l, w_in_b, cos_t, sin_t, bf_pad)


def _conv_kernel(h_ref, w_ref, b_ref, g_ref, beta_ref, o_ref, hbuf, sbuf):
    s = pl.program_id(1)
    ts = h_ref.shape[1]

    @pl.when(s == 0)
    def _():
        hbuf[0:CONV_HALO, :] = jnp.zeros((CONV_HALO, CONV_CH), F32)

    hbuf[CONV_HALO:CONV_HALO + ts, :] = h_ref[0].astype(F32)
    span = ts + CONV_HALO - SUBLANES
    for r in range(1, SUBLANES):
        sbuf[r - 1, 0:span, :] = hbuf[r:r + span, :]
    acc = jnp.zeros((ts, CONV_CH), F32) + b_ref[...]
    off = CONV_HALO - (CONV_K - 1)
    for k in range(CONV_K):
        r = (off + k) % SUBLANES
        base = off + k - r
        win = hbuf[base:base + ts, :] if r == 0 else sbuf[r - 1, base:base + ts, :]
        acc = acc + w_ref[k:k + 1, :] * win
    y = _ln(acc) * g_ref[...] + beta_ref[...]
    o_ref[0] = _silu(y).astype(BF16)
    hbuf[0:CONV_HALO, :] = hbuf[ts:ts + CONV_HALO, :]


def _conv(h, conv_w, conv_b, g, beta):
    B, S, C = h.shape
    ts = min(TS_CONV, S)
    w_pad = jnp.zeros((32, C), F32).at[:CONV_K].set(conv_w)
    vec = lambda: pl.BlockSpec((1, C), lambda b, s: (0, 0))
    return pl.pallas_call(
        _conv_kernel,
        grid=(B, S // ts),
        in_specs=[pl.BlockSpec((1, ts, C), lambda b, s: (b, s, 0)),
                  pl.BlockSpec((32, C), lambda b, s: (0, 0)), vec(), vec(), vec()],
        out_specs=pl.BlockSpec((1, ts, C), lambda b, s: (b, s, 0)),
        out_shape=jax.ShapeDtypeStruct((B, S, C), BF16),
        scratch_shapes=[pltpu.VMEM((ts + CONV_HALO, C), F32),
                        pltpu.VMEM((SUBLANES - 1, ts + CONV_HALO - SUBLANES, C), F32)],
        compiler_params=_cparams(("arbitrary", "arbitrary")),
        name="conv",
    )(h, w_pad, conv_b.reshape(1, C), g.reshape(1, C), beta.reshape(1, C))


def _ret_kernel(q_ref, k_ref, v_ref, g_ref, mask_ref, qdec_ref, kdec_ref, sdec_ref, o_ref, state):
    s = pl.program_id(1)
    ts = q_ref.shape[1]

    @pl.when(s == 0)
    def _():
        state[...] = jnp.zeros_like(state)

    lane = lax.broadcasted_iota(jnp.int32, (ts, LANES), 1)
    for p in range(RET_HEADS // 2):
        q = q_ref[0, :, p * LANES:(p + 1) * LANES]
        kp = k_ref[0, :, p * LANES:(p + 1) * LANES]
        for hh in range(2):
            h = 2 * p + hh
            own = (lane < 64) if hh == 0 else (lane >= 64)
            k = jnp.where(own, kp, jnp.zeros_like(kp))
            v = v_ref[0, :, h * RET_DV:(h + 1) * RET_DV]
            sc = _dot_nt(q, k) * mask_ref[h]
            st = state[h]
            o = _dot(sc.astype(BF16), v) + _dot(q, st.astype(BF16)) * qdec_ref[h]
            kd = (k.astype(F32) * kdec_ref[h]).astype(BF16)
            state[h] = st * sdec_ref[h] + _dot_tn(kd, v)
            ry = _ln(o)
            o_ref[0, :, h * RET_DV:(h + 1) * RET_DV] = (g_ref[0, :, h * RET_DV:(h + 1) * RET_DV].astype(F32) * ry).astype(BF16)


def _ret_tables(ts):
    log_gamma = jnp.log1p(-jnp.exp2(-5.0 - jnp.arange(RET_HEADS, dtype=F32)))
    t = jnp.arange(ts)
    dt = (t[:, None] - t[None, :]).astype(F32)
    same = (t[:, None] // CHUNK) == (t[None, :] // CHUNK)
    earlier = (t[None, :] // CHUNK) < (t[:, None] // CHUNK)
    expo = jnp.where(same, jnp.abs(dt), dt)
    w = jnp.exp(log_gamma[:, None, None] * expo[None])
    mask = jnp.where((same | earlier)[None], w, 0.0)
    tf = t.astype(F32)
    qdec = jnp.broadcast_to(jnp.exp(log_gamma[:, None] * (tf[None, :] + 1.0))[:, :, None], (RET_HEADS, ts, RET_DV))
    kdec = jnp.broadcast_to(jnp.exp(log_gamma[:, None] * (ts - 1.0 - tf[None, :]))[:, :, None], (RET_HEADS, ts, LANES))
    sdec = jnp.broadcast_to(jnp.exp(log_gamma * ts)[:, None, None], (RET_HEADS, LANES, RET_DV))
    return mask.astype(F32), qdec.astype(F32), kdec.astype(F32), sdec.astype(F32)


def _retention(rq, rk, rv, rg):
    B, S, _ = rq.shape
    ts = min(TS_RET, S)
    mask, qdec, kdec, sdec = _ret_tables(ts)
    row = lambda n: pl.BlockSpec((1, ts, n), lambda b, s: (b, s, 0))
    full = lambda a: pl.BlockSpec(a.shape, lambda b, s: (0, 0, 0))
    return pl.pallas_call(
        _ret_kernel,
        grid=(B, S // ts),
        in_specs=[row(256), row(256), row(512), row(512), full(mask), full(qdec), full(kdec), full(sdec)],
        out_specs=row(512),
        out_shape=jax.ShapeDtypeStruct((B, S, 512), BF16),
        scratch_shapes=[pltpu.VMEM((RET_HEADS, LANES, RET_DV), F32)],
        compiler_params=_cparams(("arbitrary", "arbitrary")),
        name="retention",
    )(rq, rk, rv, rg, mask, qdec, kdec, sdec)


def _fox_kernel(q_ref, k_ref, v_ref, o_ref, vaug):
    qi = pl.program_id(2)
    nh = q_ref.shape[1]
    tq = q_ref.shape[2]
    tk = tq
    S = v_ref.shape[1]

    @pl.when(qi == 0)
    def _():
        for pr in range(nh // 2):
            vaug[pr, :, 0:LANES] = v_ref[0, :, pr * LANES:(pr + 1) * LANES]
            vaug[pr, :, LANES:2 * LANES] = jnp.ones((S, LANES), BF16)

    qs = [q_ref[0, h] for h in range(nh)]

    def step(j, carry, masked):
        off = pl.multiple_of(j * tk, tk)
        new = []
        for h in range(nh):
            m, acc = carry[h]
            k = k_ref[0, h, pl.ds(off, tk), :]
            s = _dot_nt(qs[h], k)
            if masked:
                r = lax.broadcasted_iota(jnp.int32, (tq, tk), 0)
                c = lax.broadcasted_iota(jnp.int32, (tq, tk), 1)
                s = jnp.where(r >= c, s, -1e30)
            m_new = jnp.maximum(m, jnp.max(s, axis=-1, keepdims=True))
            p = jnp.exp2(s - m_new)
            alpha = jnp.exp2(m - m_new)
            acc = alpha * acc + _dot(p.astype(BF16), vaug[h // 2, pl.ds(off, tk), :])
            new.append((m_new, acc))
        return tuple(new)

    init = tuple((jnp.full((tq, 1), -1e30, F32), jnp.zeros((tq, 2 * LANES), F32)) for _ in range(nh))
    carry = lax.fori_loop(0, qi, functools.partial(step, masked=False), init)
    carry = step(qi, carry, True)
    outs = [acc[:, 0:LANES] / acc[:, LANES:LANES + 1] for _, acc in carry]
    lane = lax.broadcasted_iota(jnp.int32, (tq, LANES), 1)
    for pr in range(nh // 2):
        o_ref[0, :, pr * LANES:(pr + 1) * LANES] = jnp.where(lane < 64, outs[2 * pr], outs[2 * pr + 1]).astype(BF16)


def _fox(fq, fk, fv):
    B, H, S, _ = fq.shape
    tq = min(TQ_FOX, S)
    nh = FOX_HEADS_PER_STEP
    width = nh // 2 * LANES
    return pl.pallas_call(
        _fox_kernel,
        grid=(B, H // nh, S // tq),
        in_specs=[
            pl.BlockSpec((1, nh, tq, LANES), lambda b, g, i: (b, g, i, 0)),
            pl.BlockSpec((1, nh, S, LANES), lambda b, g, i: (b, g, 0, 0)),
            pl.BlockSpec((1, S, width), lambda b, g, i: (b, 0, g)),
        ],
        out_specs=pl.BlockSpec((1, tq, width), lambda b, g, i: (b, i, g)),
        out_shape=jax.ShapeDtypeStruct((B, S, H * FOX_DH), BF16),
        scratch_shapes=[pltpu.VMEM((nh // 2, S, 2 * LANES), BF16)],
        compiler_params=_cparams(("arbitrary", "arbitrary", "arbitrary")),
        name="fox",
    )(fq, fk, fv)


def _top2_of4(a, b, c, d):
    m1, n1 = jnp.maximum(a, b), jnp.minimum(a, b)
    m2, n2 = jnp.maximum(c, d), jnp.minimum(c, d)
    return jnp.maximum(m1, m2) + jnp.maximum(jnp.minimum(m1, m2), jnp.maximum(n1, n2))


def _route_rows(score, sel):
    gs = [_top2_of4(*sel[4 * g:4 * g + 4]) for g in range(N_GROUPS)]
    best = gs[0]
    bi = jnp.zeros_like(best, dtype=jnp.int32)
    for g in range(1, N_GROUPS):
        better = gs[g] > best
        bi = jnp.where(better, g, bi)
        best = jnp.where(better, gs[g], best)

    def pick(rows, j):
        out = rows[j]
        for g in range(1, N_GROUPS):
            out = jnp.where(bi == g, rows[4 * g + j], out)
        return out

    m = [pick(sel, j) for j in range(4)]
    sc = [pick(score, j) for j in range(4)]

    def argmax4(vals):
        bv, bj = vals[0], jnp.zeros_like(bi)
        for j in range(1, 4):
            better = vals[j] > bv
            bj = jnp.where(better, j, bj)
            bv = jnp.where(better, vals[j], bv)
        return bj

    i1 = argmax4(m)
    i2 = argmax4([jnp.where(i1 == j, -jnp.inf, m[j]) for j in range(4)])

    def take(vals, idx):
        out = vals[0]
        for j in range(1, 4):
            out = jnp.where(idx == j, vals[j], out)
        return out

    w1, w2 = take(sc, i1), take(sc, i2)
    tot = w1 + w2
    return 4 * bi + i1, 4 * bi + i2, w1 / tot, w2 / tot


def _merge_kernel(x_ref, ca_ref, ra_ref, fa_ref, ada_ref, wg_ref, bg_ref, wco_ref, wro_ref, wfo_ref, wout_ref,
                  g1_ref, b1_ref, wr_ref, br_ref, x1_ref, u2_ref, ridx_ref, rw_ref, *, alpha):
    tm, D = x_ref.shape
    sh1 = ada_ref[0, 0:1, :]
    sc1 = ada_ref[0, 1:2, :]
    g1 = ada_ref[0, 2:3, :]
    sh2 = ada_ref[0, 3:4, :]
    sc2 = ada_ref[0, 4:5, :]
    sub = min(MERGE_SUB, tm)
    for r0 in range(0, tm, sub):
        rows = slice(r0, r0 + sub)
        x = x_ref[rows, :]
        u = (_ln(x) * (1.0 + sc1) + sh1).astype(BF16)
        m = None
        for i, (a_ref, w_ref) in enumerate(((ca_ref, wco_ref), (ra_ref, wro_ref), (fa_ref, wfo_ref))):
            gate = _sigmoid(_dot(u, wg_ref[:, i * D:(i + 1) * D]) + bg_ref[:, i * D:(i + 1) * D])
            t = gate * _dot(a_ref[rows, :], w_ref[...])
            m = t if m is None else m + t
        h = _dot(m.astype(BF16), wout_ref[...])
        x1 = _ln(alpha * x + g1 * h) * g1_ref[...] + b1_ref[...]
        x1_ref[rows, :] = x1
        u2 = _ln(x1) * (1.0 + sc2) + sh2
        u2_ref[rows, :] = u2
        u_hi, u_lo = _split2(u2)
        logits = _dot(u_hi, wr_ref[0]) + _dot(u_lo, wr_ref[0]) + _dot(u_hi, wr_ref[1])
        lt = jnp.transpose(logits)
        score = [_sigmoid(lt[e:e + 1, :]) for e in range(N_EXPERTS)]
        sel = [score[e] + br_ref[e:e + 1, rows] for e in range(N_EXPERTS)]
        e1, e2, w1, w2 = _route_rows(score, sel)
        ridx_ref[0:1, rows] = e1
        ridx_ref[1:2, rows] = e2
        rw_ref[0:1, rows] = w1
        rw_ref[1:2, rows] = w2


def _merge(x2, ca, ra, fa, ada_l, wg, bg, wco, wro, wfo, wout, g1, b1, wr2, br_b, S, alpha):
    T, D = x2.shape
    tm = min(TM_PROJ, S)
    spb = S // tm
    row = lambda n: pl.BlockSpec((tm, n), lambda i: (i, 0))
    const = lambda a: pl.BlockSpec(a.shape, lambda i: (0,) * a.ndim)
    return pl.pallas_call(
        functools.partial(_merge_kernel, alpha=alpha),
        grid=(T // tm,),
        in_specs=[row(D), row(512), row(512), row(512),
                  pl.BlockSpec((1, 6, D), lambda i: (i // spb, 0, 0)),
                  const(wg), const(bg), const(wco), const(wro), const(wfo), const(wout), const(g1), const(b1),
                  const(wr2), const(br_b)],
        out_specs=[row(D), row(D), pl.BlockSpec((2, tm), lambda i: (0, i)), pl.BlockSpec((2, tm), lambda i: (0, i))],
        out_shape=[jax.ShapeDtypeStruct((T, D), F32), jax.ShapeDtypeStruct((T, D), F32),
                   jax.ShapeDtypeStruct((2, T), jnp.int32), jax.ShapeDtypeStruct((2, T), F32)],
        compiler_params=_cparams(("arbitrary",)),
        name="merge",
    )(x2, ca, ra, fa, ada_l, wg, bg, wco, wro, wfo, wout, g1, b1, wr2, br_b)


def _moe_kernel(be_ref, tok_cur_ref, tok_nxt_ref, sd_prev_ref, sd_cur_ref, x_hbm, w1_ref, w3_ref, w2_ref, y_hbm,
                xbuf, ybuf, xb, gsem, ssem):
    i = pl.program_id(0)
    nb = pl.num_programs(0)
    tm = xbuf.shape[1]
    slot = i % 2
    other = 1 - slot

    def gather_row(tok_ref, r, dst):
        pltpu.make_async_copy(x_hbm.at[pl.ds(tok_ref[0, r], 1), :], xbuf.at[dst, pl.ds(r, 1), :], gsem.at[dst]).start()

    def scatter_row(sd_ref, r, src):
        pltpu.make_async_copy(ybuf.at[src, pl.ds(r, 1), :], y_hbm.at[pl.ds(sd_ref[0, r], 1), :],
                              ssem.at[src]).start(priority=1)

    def wait_rows(sem, buf):
        pltpu.make_async_copy(x_hbm.at[pl.ds(0, tm), :], buf, sem).wait()

    @pl.when(i == 0)
    def _():
        ybuf[...] = jnp.zeros_like(ybuf)

        def body(r, c):
            gather_row(tok_cur_ref, r, 0)
            return c
        lax.fori_loop(0, tm, body, 0)

    wait_rows(gsem.at[slot], xbuf.at[slot])
    xb[...] = xbuf[slot].astype(BF16)
    x = xb[...]
    n_chunk = w1_ref.shape[2] // FF_CHUNK
    bounds = [tm * c // (n_chunk - 1) for c in range(n_chunk)] + [tm]
    acc = None
    for c in range(n_chunk):
        for r in range(bounds[c], bounds[c + 1]):
            gather_row(tok_nxt_ref, r, other)
            scatter_row(sd_prev_ref, r, other)
        c0 = c * FF_CHUNK
        h1 = _dot(x, w1_ref[0, :, c0:c0 + FF_CHUNK])
        h3 = _dot(x, w3_ref[0, :, c0:c0 + FF_CHUNK])
        h = (_silu(h1) * h3).astype(BF16)
        t = _dot(h, w2_ref[0, c0:c0 + FF_CHUNK, :])
        acc = t if acc is None else acc + t
        pl.delay(1)

    @pl.when(i >= 1)
    def _():
        wait_rows(ssem.at[slot], ybuf.at[slot])

    ybuf[slot] = acc

    @pl.when(i == nb - 1)
    def _():
        wait_rows(ssem.at[other], ybuf.at[other])

        def body(r, c):
            scatter_row(sd_cur_ref, r, slot)
            return c
        lax.fori_loop(0, tm, body, 0)
        wait_rows(ssem.at[slot], ybuf.at[slot])
        wait_rows(gsem.at[other], xbuf.at[other])


def _moe(u2, buf_tok, sdest, block_e, w1b, w3b, w2b, n_out_rows):
    T, D = u2.shape
    E, _, FF = w1b.shape
    nb = buf_tok.shape[0]
    tm = buf_tok.shape[2]
    idx_spec = lambda f: pl.BlockSpec((None, 1, tm), f, memory_space=pltpu.SMEM)
    grid_spec = pltpu.PrefetchScalarGridSpec(
        num_scalar_prefetch=1,
        grid=(nb,),
        in_specs=[
            idx_spec(lambda i, be: (i, 0, 0)),
            idx_spec(lambda i, be: (jnp.minimum(i + 1, nb - 1), 0, 0)),
            idx_spec(lambda i, be: (i, 0, 0)),
            idx_spec(lambda i, be: (i + 1, 0, 0)),
            pl.BlockSpec(memory_space=pl.ANY),
            pl.BlockSpec((1, D, FF), lambda i, be: (be[i], 0, 0)),
            pl.BlockSpec((1, D, FF), lambda i, be: (be[i], 0, 0)),
            pl.BlockSpec((1, FF, D), lambda i, be: (be[i], 0, 0)),
        ],
        out_specs=pl.BlockSpec(memory_space=pl.ANY),
        scratch_shapes=[pltpu.VMEM((2, tm, D), F32), pltpu.VMEM((2, tm, D), F32), pltpu.VMEM((tm, D), BF16),
                        pltpu.SemaphoreType.DMA((2,)), pltpu.SemaphoreType.DMA((2,))],
    )
    return pl.pallas_call(
        _moe_kernel,
        grid_spec=grid_spec,
        out_shape=jax.ShapeDtypeStruct((n_out_rows, D), F32),
        compiler_params=_cparams(("arbitrary",)),
        name="moe",
    )(block_e, buf_tok, buf_tok, sdest, sdest, u2, w1b, w3b, w2b)


def _combine_kernel(y0_ref, y1_ref, x1_ref, w_ref, ada_ref, g_ref, b_ref, o_ref, *, alpha):
    g2 = ada_ref[0, 5:6, :]
    w = w_ref[...]
    h2 = w[:, 0:1] * y0_ref[...] + w[:, 1:2] * y1_ref[...]
    o_ref[...] = _ln(alpha * x1_ref[...] + g2 * h2) * g_ref[...] + b_ref[...]


def _combine(y_tok, x1, wts, ada_l, g2v, b2v, S, alpha):
    T, D = x1.shape
    tm = min(TM_COMB, S)
    spb = S // tm
    nb = T // tm
    const = lambda a: pl.BlockSpec(a.shape, lambda i: (0,) * a.ndim)
    return pl.pallas_call(
        functools.partial(_combine_kernel, alpha=alpha),
        grid=(nb,),
        in_specs=[
            pl.BlockSpec((tm, D), lambda i: (i, 0)),
            pl.BlockSpec((tm, D), lambda i: (i + nb, 0)),
            pl.BlockSpec((tm, D), lambda i: (i, 0)),
            pl.BlockSpec((tm, TOP_K), lambda i: (i, 0)),
            pl.BlockSpec((1, 6, D), lambda i: (i // spb, 0, 0)),
            const(g2v), const(b2v),
        ],
        out_specs=pl.BlockSpec((tm, D), lambda i: (i, 0)),
        out_shape=jax.ShapeDtypeStruct((T, D), F32),
        compiler_params=_cparams(("arbitrary",)),
        name="combine",
    )(y_tok, y_tok, x1, wts, ada_l, g2v, b2v)


def _plan(ridx, tm):
    _, T = ridx.shape
    A = TOP_K * T
    nb = A // tm + N_EXPERTS
    P = nb * tm
    flat_e = ridx.reshape(A)
    experts = jnp.arange(N_EXPERTS, dtype=jnp.int32)
    counts = jnp.sum((flat_e[:, None] == experts[None, :]).astype(jnp.int32), axis=0)
    start = jnp.cumsum(counts) - counts
    padded = (counts + tm - 1) // tm * tm
    pend = jnp.cumsum(padded)
    pstart = pend - padded
    by_expert = jnp.sort(flat_e * A + jnp.arange(A, dtype=jnp.int32)) % A
    block_starts = jnp.arange(nb, dtype=jnp.int32) * tm
    block_e = jnp.minimum(jnp.sum((pend[None, :] <= block_starts[:, None]).astype(jnp.int32), axis=1),
                          N_EXPERTS - 1).astype(jnp.int32)
    p = jnp.arange(P, dtype=jnp.int32)
    e_p = jnp.repeat(block_e, tm)
    j = p - pstart[e_p]
    valid = j < counts[e_p]
    a_p = by_expert[jnp.clip(start[e_p] + j, 0, A - 1)]
    buf_tok = jnp.where(valid, a_p % T, 0).astype(jnp.int32)
    sdest = jnp.where(valid, a_p, A + p - start[e_p] - counts[e_p]).astype(jnp.int32)
    sdest = jnp.concatenate([P + jnp.arange(tm, dtype=jnp.int32), sdest])
    return buf_tok.reshape(nb, 1, tm), sdest.reshape(nb + 1, 1, tm), block_e, P + tm


def _rotary_tables(S):
    half = RET_DK // 2
    inv = RET_THETA ** (-jnp.arange(half, dtype=F32) / half)
    ang = jnp.arange(S, dtype=F32)[:, None] * inv[None, :]
    cos, sin = jnp.cos(ang), jnp.sin(ang)
    cos_t = jnp.tile(jnp.concatenate([cos, cos], axis=1), (1, LANES // RET_DK))
    sin_t = jnp.tile(jnp.concatenate([-sin, sin], axis=1), (1, LANES // RET_DK))
    return cos_t, sin_t


def kernel(x, c, w_ada, b_ada, w_in, conv_w, conv_b, conv_ln_g, conv_ln_b, b_forget, w_conv_out, w_ret_out,
           w_fox_out, w_gate, b_gate, w_out, ln1_g, ln1_b, w_router, b_router, w1, w3, w2, ln2_g, ln2_b):
    B, S, D = x.shape
    L = w_ada.shape[0]
    T = B * S
    alpha = (2 * L) ** 0.25

    ada = _ada(c, w_ada, b_ada).reshape(L, B, 6, D)
    cos_t, sin_t = _rotary_tables(S)

    wr = jnp.zeros((D, LANES), F32).at[:, :N_EXPERTS].set(w_router)
    wr_hi = wr.astype(BF16)
    wr_lo = (wr - wr_hi.astype(F32)).astype(BF16)
    wr2 = jnp.stack([wr_hi, wr_lo])
    br_b = jnp.broadcast_to(b_router.astype(F32)[:, None], (N_EXPERTS, min(TM_PROJ, S)))

    for l in range(L):
        w_in_b = jnp.zeros((D, IN_COLS_PAD), BF16).at[:, :w_in.shape[2]].set(w_in[l].astype(BF16))
        bf_pad = jnp.zeros((1, LANES), F32).at[0, :FOX_HEADS].set(b_forget[l])
        h, rq, rk, rv, rg, fq, fk, fv = _inproj(x, ada[l], w_in_b, cos_t, sin_t, bf_pad)
        conv_act = _conv(h, conv_w[l], conv_b[l], conv_ln_g[l], conv_ln_b[l])
        ret_act = _retention(rq, rk, rv, rg)
        fox_act = _fox(fq, fk, fv)
        x1, u2, ridx, rw = _merge(
            x.reshape(T, D), conv_act.reshape(T, 512), ret_act.reshape(T, 512), fox_act.reshape(T, 512), ada[l],
            w_gate[l].astype(BF16), b_gate[l].reshape(1, -1), w_conv_out[l].astype(BF16), w_ret_out[l].astype(BF16),
            w_fox_out[l].astype(BF16), w_out[l].astype(BF16), ln1_g[l].reshape(1, D), ln1_b[l].reshape(1, D),
            wr2, br_b, S, alpha)
        buf_tok, sdest, block_e, n_out_rows = _plan(ridx, TM_MOE)
        y_tok = _moe(u2, buf_tok, sdest, block_e, w1[l].astype(BF16), w3[l].astype(BF16), w2[l].astype(BF16),
                     n_out_rows)
        x = _combine(y_tok, x1, rw.T, ada[l], ln2_g[l].reshape(1, D), ln2_b[l].reshape(1, D), S,
                     alpha).reshape(B, S, D)
    return x
```

```python
import functools

import jax
import jax.numpy as jnp
import numpy as np
from jax import lax
from jax.experimental import pallas as pl
from jax.experimental.pallas import tpu as pltpu

F32 = jnp.float32
BF16 = jnp.bfloat16

CHUNK = 64
CONV_CH = 512
CONV_K = 31
RET_HEADS = 4
RET_DK = 64
RET_DV = 128
RET_THETA = 10000.0
FOX_HEADS = 8
FOX_DH = 64
N_EXPERTS = 16
N_GROUPS = 4
EXPERTS_PER_GROUP = 4
TOP_K = 2
LN_EPS = 1e-5
LOG2E = 1.4426950408889634

A_VAL = 0
A_GATE = 512
R_Q = 1024
R_K = 1280
R_V = 1536
R_G = 2048
F_Q = 2560
F_K = 3072
F_V = 3584
F_F = 4096
IN_COLS_PAD = 4224

LANES = 128
SUBLANES = 8
CONV_HALO = 32

TM_PROJ = 512
TS_CONV = 512
TS_RET = 256
TQ_FOX = 512
FOX_HEADS_PER_STEP = 4
TM_MOE = 512
FF_CHUNK = 512
TM_COMB = 512
MERGE_SUB = 256
PROJ_SUB = 256

VMEM_LIMIT = 56 * 1024 * 1024


def _cparams(sem, vmem=VMEM_LIMIT):
    return pltpu.CompilerParams(dimension_semantics=sem, vmem_limit_bytes=vmem)


def _ln(x):
    mu = jnp.mean(x, axis=-1, keepdims=True)
    xc = x - mu
    var = jnp.mean(xc * xc, axis=-1, keepdims=True)
    return xc * lax.rsqrt(var + LN_EPS)


def _sigmoid(x):
    return 1.0 / (1.0 + jnp.exp(-x))


def _silu(x):
    return x * _sigmoid(x)


def _split2(x):
    hi = x.astype(BF16)
    lo = (x - hi.astype(F32)).astype(BF16)
    return hi, lo


def _split3(x):
    c1 = x.astype(BF16).astype(F32)
    r = x - c1
    c2 = r.astype(BF16).astype(F32)
    c3 = (r - c2).astype(BF16).astype(F32)
    return c1, c2, c3


def _dot(a, b):
    return jnp.dot(a, b, preferred_element_type=F32)


def _dot_nt(a, b):
    return lax.dot_general(a, b, (((1,), (1,)), ((), ())), preferred_element_type=F32)


def _dot_tn(a, b):
    return lax.dot_general(a, b, (((0,), (0,)), ((), ())), preferred_element_type=F32)


def _ada_kernel(c_ref, w_ref, b_ref, o_ref):
    sc = _silu(c_ref[...])
    a_hi, a_lo = _split2(sc)
    w = w_ref[0]
    w_hi, w_lo = _split2(w)
    o_ref[0] = _dot(a_hi, w_hi) + _dot(a_lo, w_hi) + _dot(a_hi, w_lo) + b_ref[0]


def _ada(c, w_ada, b_ada):
    L, D, N = w_ada.shape
    B = c.shape[0]
    tn = 1536
    return pl.pallas_call(
        _ada_kernel,
        grid=(L, N // tn),
        in_specs=[
            pl.BlockSpec((B, D), lambda l, j: (0, 0)),
            pl.BlockSpec((1, D, tn), lambda l, j: (l, 0, j)),
            pl.BlockSpec((1, 1, tn), lambda l, j: (l, 0, j)),
        ],
        out_specs=pl.BlockSpec((1, B, tn), lambda l, j: (l, 0, j)),
        out_shape=jax.ShapeDtypeStruct((L, B, N), F32),
        compiler_params=_cparams(("arbitrary", "arbitrary")),
        name="ada",
    )(c, w_ada, b_ada.reshape(L, 1, N))


def _rotary(z, cos, sin_signed, lane):
    partner = jnp.where((lane & 63) < 32, pltpu.roll(z, 96, axis=1), pltpu.roll(z, 32, axis=1))
    return z * cos + partner * sin_signed


def _inproj_kernel(x_ref, ada_ref, w_ref, cos_ref, sin_ref, bf_ref,
                   h_ref, rq_ref, rk_ref, rv_ref, rg_ref, fq_ref, fk_ref, fv_ref, carry_ref):
    s = pl.program_id(1)
    tm = x_ref.shape[1]

    @pl.when(s == 0)
    def _():
        carry_ref[...] = jnp.zeros_like(carry_ref)

    sh1 = ada_ref[0, 0:1, :]
    sc1 = ada_ref[0, 1:2, :]
    sub = min(PROJ_SUB, tm)
    lane = lax.broadcasted_iota(jnp.int32, (sub, LANES), 1)
    row = lax.broadcasted_iota(jnp.int32, (sub, LANES), 0)
    one_q = jnp.where((lane >= 67) & (lane < 70), 1.0, 0.0)
    one_k = jnp.where((lane >= 64) & (lane < 67), 1.0, 0.0)
    for r0 in range(0, tm, sub):
        rows = slice(r0, r0 + sub)
        u = (_ln(x_ref[0, rows, :]) * (1.0 + sc1) + sh1).astype(BF16)

        def proj(c0, c1):
            return _dot(u, w_ref[:, c0:c1])

        h_ref[0, rows, :] = (proj(A_VAL, A_GATE) * _sigmoid(proj(A_GATE, R_Q))).astype(BF16)

        cos = cos_ref[rows, :]
        sin = sin_ref[rows, :]
        z_rq = proj(R_Q, R_K)
        z_rk = proj(R_K, R_V)
        for p in range(2):
            cols = slice(p * LANES, (p + 1) * LANES)
            rq_ref[0, rows, cols] = _rotary(z_rq[:, cols], cos, sin, lane).astype(BF16)
            rk = _rotary(z_rk[:, cols], cos, sin, lane) * (RET_DK ** -0.5)
            rk_ref[0, rows, cols] = rk.astype(BF16)
        rv_ref[0, rows, :] = proj(R_V, R_G).astype(BF16)
        rg_ref[0, rows, :] = _silu(proj(R_G, F_Q)).astype(BF16)
        z_fv = proj(F_V, IN_COLS_PAD)
        fv_ref[0, rows, :] = z_fv[:, 0:F_F - F_V].astype(BF16)

        f_logit = z_fv[:, F_F - F_V:] + bf_ref[...]
        log_f = jnp.minimum(f_logit, 0.0) - jnp.log(1.0 + jnp.exp(-jnp.abs(f_logit)))
        y = log_f
        sh = 1
        while sh < sub:
            y = y + jnp.where(row >= sh, pltpu.roll(y, sh, axis=0), 0.0)
            sh *= 2
        cum = y + carry_ref[0:1, :]
        carry_ref[0:1, :] = cum[sub - 1:sub, :]
        c1, c2, c3 = _split3(cum * LOG2E)

        z_fq = proj(F_Q, F_K)
        z_fk = proj(F_K, F_V)
        for p in range(FOX_HEADS // 2):
            zq = z_fq[:, p * LANES:(p + 1) * LANES] * (FOX_DH ** -0.5 * LOG2E)
            zk = z_fk[:, p * LANES:(p + 1) * LANES]
            zq_sw = pltpu.roll(zq, 64, axis=1)
            zk_sw = pltpu.roll(zk, 64, axis=1)
            for hh in range(2):
                h = 2 * p + hh
                a1 = c1[:, h:h + 1]
                a2 = c2[:, h:h + 1]
                a3 = c3[:, h:h + 1]
                ext_q = jnp.where(lane == 64, a1, jnp.where(lane == 65, a2, jnp.where(lane == 66, a3, one_q)))
                ext_k = jnp.where(lane == 67, -a1, jnp.where(lane == 68, -a2, jnp.where(lane == 69, -a3, one_k)))
                fq_ref[0, h, rows, :] = jnp.where(lane < 64, zq if hh == 0 else zq_sw, ext_q).astype(BF16)
                fk_ref[0, h, rows, :] = jnp.where(lane < 64, zk if hh == 0 else zk_sw, ext_k).astype(BF16)


def _inproj(x, ada_l, w_in_b, cos_t, sin_t, bf_pad):
    B, S, D = x.shape
    tm = min(TM_PROJ, S)
    bsd = lambda n, dt: jax.ShapeDtypeStruct((B, S, n), dt)
    row_spec = lambda n: pl.BlockSpec((1, tm, n), lambda b, s: (b, s, 0))
    aug_spec = pl.BlockSpec((1, FOX_HEADS, tm, LANES), lambda b, s: (b, 0, s, 0))
    aug_shape = jax.ShapeDtypeStruct((B, FOX_HEADS, S, LANES), BF16)
    return pl.pallas_call(
        _inproj_kernel,
        grid=(B, S // tm),
        in_specs=[
            row_spec(D),
            pl.BlockSpec((1, 6, D), lambda b, s: (b, 0, 0)),
            pl.BlockSpec((D, IN_COLS_PAD), lambda b, s: (0, 0)),
            pl.BlockSpec((tm, LANES), lambda b, s: (s, 0)),
            pl.BlockSpec((tm, LANES), lambda b, s: (s, 0)),
            pl.BlockSpec((1, LANES), lambda b, s: (0, 0)),
        ],
        out_specs=[row_spec(CONV_CH), row_spec(256), row_spec(256), row_spec(512), row_spec(512),
                   aug_spec, aug_spec, row_spec(512)],
        out_shape=[bsd(CONV_CH, BF16), bsd(256, BF16), bsd(256, BF16), bsd(512, BF16), bsd(512, BF16),
                   aug_shape, aug_shape, bsd(512, BF16)],
        scratch_shapes=[pltpu.VMEM((8, LANES), F32)],
        compiler_params=_cparams(("arbitrary", "arbitrary")),
        name="inproj",
    )(x, ada_l, w_in_b, cos_t, sin_t, bf_pad)


def _conv_kernel(h_ref, w_ref, b_ref, g_ref, beta_ref, o_ref, hbuf, sbuf):
    s = pl.program_id(1)
    ts = h_ref.shape[1]

    @pl.when(s == 0)
    def _():
        hbuf[0:CONV_HALO, :] = jnp.zeros((CONV_HALO, CONV_CH), F32)

    hbuf[CONV_HALO:CONV_HALO + ts, :] = h_ref[0].astype(F32)
    span = ts + CONV_HALO - SUBLANES
    for r in range(1, SUBLANES):
        sbuf[r - 1, 0:span, :] = hbuf[r:r + span, :]
    acc = jnp.zeros((ts, CONV_CH), F32) + b_ref[...]
    off = CONV_HALO - (CONV_K - 1)
    for k in range(CONV_K):
        r = (off + k) % SUBLANES
        base = off + k - r
        win = hbuf[base:base + ts, :] if r == 0 else sbuf[r - 1, base:base + ts, :]
        acc = acc + w_ref[k:k + 1, :] * win
    y = _ln(acc) * g_ref[...] + beta_ref[...]
    o_ref[0] = _silu(y).astype(BF16)
    hbuf[0:CONV_HALO, :] = hbuf[ts:ts + CONV_HALO, :]


def _conv(h, conv_w, conv_b, g, beta):
    B, S, C = h.shape
    ts = min(TS_CONV, S)
    w_pad = jnp.zeros((32, C), F32).at[:CONV_K].set(conv_w)
    vec = lambda: pl.BlockSpec((1, C), lambda b, s: (0, 0))
    return pl.pallas_call(
        _conv_kernel,
        grid=(B, S // ts),
        in_specs=[pl.BlockSpec((1, ts, C), lambda b, s: (b, s, 0)),
                  pl.BlockSpec((32, C), lambda b, s: (0, 0)), vec(), vec(), vec()],
        out_specs=pl.BlockSpec((1, ts, C), lambda b, s: (b, s, 0)),
        out_shape=jax.ShapeDtypeStruct((B, S, C), BF16),
        scratch_shapes=[pltpu.VMEM((ts + CONV_HALO, C), F32),
                        pltpu.VMEM((SUBLANES - 1, ts + CONV_HALO - SUBLANES, C), F32)],
        compiler_params=_cparams(("arbitrary", "arbitrary")),
        name="conv",
    )(h, w_pad, conv_b.reshape(1, C), g.reshape(1, C), beta.reshape(1, C))


def _ret_kernel(q_ref, k_ref, v_ref, g_ref, mask_ref, qdec_ref, kdec_ref, sdec_ref, o_ref, state):
    s = pl.program_id(1)
    ts = q_ref.shape[1]

    @pl.when(s == 0)
    def _():
        state[...] = jnp.zeros_like(state)

    lane = lax.broadcasted_iota(jnp.int32, (ts, LANES), 1)
    for p in range(RET_HEADS // 2):
        q = q_ref[0, :, p * LANES:(p + 1) * LANES]
        kp = k_ref[0, :, p * LANES:(p + 1) * LANES]
        for hh in range(2):
            h = 2 * p + hh
            own = (lane < 64) if hh == 0 else (lane >= 64)
            k = jnp.where(own, kp, jnp.zeros_like(kp))
            v = v_ref[0, :, h * RET_DV:(h + 1) * RET_DV]
            sc = _dot_nt(q, k) * mask_ref[h]
            st = state[h]
            o = _dot(sc.astype(BF16), v) + _dot(q, st.astype(BF16)) * qdec_ref[h]
            kd = (k.astype(F32) * kdec_ref[h]).astype(BF16)
            state[h] = st * sdec_ref[h] + _dot_tn(kd, v)
            ry = _ln(o)
            o_ref[0, :, h * RET_DV:(h + 1) * RET_DV] = (g_ref[0, :, h * RET_DV:(h + 1) * RET_DV].astype(F32) * ry).astype(BF16)


def _ret_tables(ts):
    log_gamma = jnp.log1p(-jnp.exp2(-5.0 - jnp.arange(RET_HEADS, dtype=F32)))
    t = jnp.arange(ts)
    dt = (t[:, None] - t[None, :]).astype(F32)
    same = (t[:, None] // CHUNK) == (t[None, :] // CHUNK)
    earlier = (t[None, :] // CHUNK) < (t[:, None] // CHUNK)
    expo = jnp.where(same, jnp.abs(dt), dt)
    w = jnp.exp(log_gamma[:, None, None] * expo[None])
    mask = jnp.where((same | earlier)[None], w, 0.0)
    tf = t.astype(F32)
    qdec = jnp.broadcast_to(jnp.exp(log_gamma[:, None] * (tf[None, :] + 1.0))[:, :, None], (RET_HEADS, ts, RET_DV))
    kdec = jnp.broadcast_to(jnp.exp(log_gamma[:, None] * (ts - 1.0 - tf[None, :]))[:, :, None], (RET_HEADS, ts, LANES))
    sdec = jnp.broadcast_to(jnp.exp(log_gamma * ts)[:, None, None], (RET_HEADS, LANES, RET_DV))
    return mask.astype(F32), qdec.astype(F32), kdec.astype(F32), sdec.astype(F32)


def _retention(rq, rk, rv, rg):
    B, S, _ = rq.shape
    ts = min(TS_RET, S)
    mask, qdec, kdec, sdec = _ret_tables(ts)
    row = lambda n: pl.BlockSpec((1, ts, n), lambda b, s: (b, s, 0))
    full = lambda a: pl.BlockSpec(a.shape, lambda b, s: (0, 0, 0))
    return pl.pallas_call(
        _ret_kernel,
        grid=(B, S // ts),
        in_specs=[row(256), row(256), row(512), row(512), full(mask), full(qdec), full(kdec), full(sdec)],
        out_specs=row(512),
        out_shape=jax.ShapeDtypeStruct((B, S, 512), BF16),
        scratch_shapes=[pltpu.VMEM((RET_HEADS, LANES, RET_DV), F32)],
        compiler_params=_cparams(("arbitrary", "arbitrary")),
        name="retention",
    )(rq, rk, rv, rg, mask, qdec, kdec, sdec)


def _fox_kernel(q_ref, k_ref, v_ref, o_ref, vaug):
    qi = pl.program_id(2)
    nh = q_ref.shape[1]
    tq = q_ref.shape[2]
    tk = tq
    S = v_ref.shape[1]

    @pl.when(qi == 0)
    def _():
        for pr in range(nh // 2):
            vaug[pr, :, 0:LANES] = v_ref[0, :, pr * LANES:(pr + 1) * LANES]
            vaug[pr, :, LANES:2 * LANES] = jnp.ones((S, LANES), BF16)

    qs = [q_ref[0, h] for h in range(nh)]

    def step(off, width, q_rows, carry, masked):
        new = []
        for h in range(nh):
            m, acc = carry[h]
            k = k_ref[0, h, pl.ds(off, width), :]
            s = _dot_nt(q_rows[h], k)
            if masked:
                r = lax.broadcasted_iota(jnp.int32, s.shape, 0)
                c = lax.broadcasted_iota(jnp.int32, s.shape, 1)
                s = jnp.where(r >= c, s, -1e30)
            m_new = jnp.maximum(m, jnp.max(s, axis=-1, keepdims=True))
            p = jnp.exp2(s - m_new)
            alpha = jnp.exp2(m - m_new)
            acc = alpha * acc + _dot(p.astype(BF16), vaug[h // 2, pl.ds(off, width), :])
            new.append((m_new, acc))
        return tuple(new)

    init = tuple((jnp.full((tq, 1), -1e30, F32), jnp.zeros((tq, 2 * LANES), F32)) for _ in range(nh))
    carry = lax.fori_loop(0, qi, lambda j, c: step(pl.multiple_of(j * tk, tk), tk, qs, c, False), init)
    carry = step(pl.multiple_of(qi * tq, tq), tk, qs, carry, True)
    outs = [acc[:, 0:LANES] / acc[:, LANES:LANES + 1] for _, acc in carry]
    lane = lax.broadcasted_iota(jnp.int32, (tq, LANES), 1)
    for pr in range(nh // 2):
        o_ref[0, :, pr * LANES:(pr + 1) * LANES] = jnp.where(lane < 64, outs[2 * pr], outs[2 * pr + 1]).astype(BF16)


def _fox(fq, fk, fv):
    B, H, S, _ = fq.shape
    tq = min(TQ_FOX, S)
    nh = FOX_HEADS_PER_STEP
    width = nh // 2 * LANES
    return pl.pallas_call(
        _fox_kernel,
        grid=(B, H // nh, S // tq),
        in_specs=[
            pl.BlockSpec((1, nh, tq, LANES), lambda b, g, i: (b, g, i, 0)),
            pl.BlockSpec((1, nh, S, LANES), lambda b, g, i: (b, g, 0, 0)),
            pl.BlockSpec((1, S, width), lambda b, g, i: (b, 0, g)),
        ],
        out_specs=pl.BlockSpec((1, tq, width), lambda b, g, i: (b, i, g)),
        out_shape=jax.ShapeDtypeStruct((B, S, H * FOX_DH), BF16),
        scratch_shapes=[pltpu.VMEM((nh // 2, S, 2 * LANES), BF16)],
        compiler_params=_cparams(("arbitrary", "arbitrary", "arbitrary")),
        name="fox",
    )(fq, fk, fv)


def _top2_of4(a, b, c, d):
    m1, n1 = jnp.maximum(a, b), jnp.minimum(a, b)
    m2, n2 = jnp.maximum(c, d), jnp.minimum(c, d)
    return jnp.maximum(m1, m2) + jnp.maximum(jnp.minimum(m1, m2), jnp.maximum(n1, n2))


def _route_rows(score, sel):
    gs = [_top2_of4(*sel[4 * g:4 * g + 4]) for g in range(N_GROUPS)]
    best = gs[0]
    bi = jnp.zeros_like(best, dtype=jnp.int32)
    for g in range(1, N_GROUPS):
        better = gs[g] > best
        bi = jnp.where(better, g, bi)
        best = jnp.where(better, gs[g], best)

    def pick(rows, j):
        out = rows[j]
        for g in range(1, N_GROUPS):
            out = jnp.where(bi == g, rows[4 * g + j], out)
        return out

    m = [pick(sel, j) for j in range(4)]
    sc = [pick(score, j) for j in range(4)]

    def argmax4(vals):
        bv, bj = vals[0], jnp.zeros_like(bi)
        for j in range(1, 4):
            better = vals[j] > bv
            bj = jnp.where(better, j, bj)
            bv = jnp.where(better, vals[j], bv)
        return bj

    i1 = argmax4(m)
    i2 = argmax4([jnp.where(i1 == j, -jnp.inf, m[j]) for j in range(4)])

    def take(vals, idx):
        out = vals[0]
        for j in range(1, 4):
            out = jnp.where(idx == j, vals[j], out)
        return out

    w1, w2 = take(sc, i1), take(sc, i2)
    tot = w1 + w2
    return 4 * bi + i1, 4 * bi + i2, w1 / tot, w2 / tot


def _merge_kernel(x_ref, ca_ref, ra_ref, fa_ref, ada_ref, wg_ref, bg_ref, wco_ref, wro_ref, wfo_ref, wout_ref,
                  g1_ref, b1_ref, wr_ref, br_ref, x1_ref, u2_ref, ridx_ref, rw_ref, *, alpha):
    tm, D = x_ref.shape
    sh1 = ada_ref[0, 0:1, :]
    sc1 = ada_ref[0, 1:2, :]
    g1 = ada_ref[0, 2:3, :]
    sh2 = ada_ref[0, 3:4, :]
    sc2 = ada_ref[0, 4:5, :]
    sub = min(MERGE_SUB, tm)
    for r0 in range(0, tm, sub):
        rows = slice(r0, r0 + sub)
        x = x_ref[rows, :]
        u = (_ln(x) * (1.0 + sc1) + sh1).astype(BF16)
        m = None
        for i, (a_ref, w_ref) in enumerate(((ca_ref, wco_ref), (ra_ref, wro_ref), (fa_ref, wfo_ref))):
            gate = _sigmoid(_dot(u, wg_ref[:, i * D:(i + 1) * D]) + bg_ref[:, i * D:(i + 1) * D])
            t = gate * _dot(a_ref[rows, :], w_ref[...])
            m = t if m is None else m + t
        h = _dot(m.astype(BF16), wout_ref[...])
        x1 = _ln(alpha * x + g1 * h) * g1_ref[...] + b1_ref[...]
        x1_ref[rows, :] = x1
        u2 = _ln(x1) * (1.0 + sc2) + sh2
        u2_ref[rows, :] = u2
        u_hi, u_lo = _split2(u2)
        logits = _dot(u_hi, wr_ref[0]) + _dot(u_lo, wr_ref[0]) + _dot(u_hi, wr_ref[1])
        lt = jnp.transpose(logits)
        score = [_sigmoid(lt[e:e + 1, :]) for e in range(N_EXPERTS)]
        sel = [score[e] + br_ref[e:e + 1, rows] for e in range(N_EXPERTS)]
        e1, e2, w1, w2 = _route_rows(score, sel)
        ridx_ref[0:1, rows] = e1
        ridx_ref[1:2, rows] = e2
        rw_ref[0:1, rows] = w1
        rw_ref[1:2, rows] = w2


def _merge(x2, ca, ra, fa, ada_l, wg, bg, wco, wro, wfo, wout, g1, b1, wr2, br_b, S, alpha):
    T, D = x2.shape
    tm = min(TM_PROJ, S)
    spb = S // tm
    row = lambda n: pl.BlockSpec((tm, n), lambda i: (i, 0))
    const = lambda a: pl.BlockSpec(a.shape, lambda i: (0,) * a.ndim)
    return pl.pallas_call(
        functools.partial(_merge_kernel, alpha=alpha),
        grid=(T // tm,),
        in_specs=[row(D), row(512), row(512), row(512),
                  pl.BlockSpec((1, 6, D), lambda i: (i // spb, 0, 0)),
                  const(wg), const(bg), const(wco), const(wro), const(wfo), const(wout), const(g1), const(b1),
                  const(wr2), const(br_b)],
        out_specs=[row(D), row(D), pl.BlockSpec((2, tm), lambda i: (0, i)), pl.BlockSpec((2, tm), lambda i: (0, i))],
        out_shape=[jax.ShapeDtypeStruct((T, D), F32), jax.ShapeDtypeStruct((T, D), F32),
                   jax.ShapeDtypeStruct((2, T), jnp.int32), jax.ShapeDtypeStruct((2, T), F32)],
        compiler_params=_cparams(("arbitrary",)),
        name="merge",
    )(x2, ca, ra, fa, ada_l, wg, bg, wco, wro, wfo, wout, g1, b1, wr2, br_b)


def _moe_kernel(be_ref, tok_cur_ref, tok_nxt_ref, sd_prev_ref, sd_cur_ref, x_hbm, w1_ref, w3_ref, w2_ref, y_hbm,
                xbuf, ybuf, xb, gsem, ssem):
    i = pl.program_id(0)
    nb = pl.num_programs(0)
    tm = xbuf.shape[1]
    slot = i % 2
    other = 1 - slot

    def gather_row(tok_ref, r, dst):
        pltpu.make_async_copy(x_hbm.at[pl.ds(tok_ref[0, r], 1), :], xbuf.at[dst, pl.ds(r, 1), :], gsem.at[dst]).start()

    def scatter_row(sd_ref, r, src):
        pltpu.make_async_copy(ybuf.at[src, pl.ds(r, 1), :], y_hbm.at[pl.ds(sd_ref[0, r], 1), :],
                              ssem.at[src]).start(priority=1)

    def wait_rows(sem, buf):
        pltpu.make_async_copy(x_hbm.at[pl.ds(0, tm), :], buf, sem).wait()

    @pl.when(i == 0)
    def _():
        ybuf[...] = jnp.zeros_like(ybuf)

        def body(r, c):
            gather_row(tok_cur_ref, r, 0)
            return c
        lax.fori_loop(0, tm, body, 0)

    wait_rows(gsem.at[slot], xbuf.at[slot])
    xb[...] = xbuf[slot].astype(BF16)
    x = xb[...]
    n_chunk = w1_ref.shape[2] // FF_CHUNK
    bounds = [tm * c // (n_chunk - 1) for c in range(n_chunk)] + [tm]
    acc = None
    for c in range(n_chunk):
        for r in range(bounds[c], bounds[c + 1]):
            gather_row(tok_nxt_ref, r, other)
            scatter_row(sd_prev_ref, r, other)
        c0 = c * FF_CHUNK
        h1 = _dot(x, w1_ref[0, :, c0:c0 + FF_CHUNK])
        h3 = _dot(x, w3_ref[0, :, c0:c0 + FF_CHUNK])
        h = (_silu(h1) * h3).astype(BF16)
        t = _dot(h, w2_ref[0, c0:c0 + FF_CHUNK, :])
        acc = t if acc is None else acc + t
        pl.delay(1)

    @pl.when(i >= 1)
    def _():
        wait_rows(ssem.at[slot], ybuf.at[slot])

    ybuf[slot] = acc

    @pl.when(i == nb - 1)
    def _():
        wait_rows(ssem.at[other], ybuf.at[other])

        def body(r, c):
            scatter_row(sd_cur_ref, r, slot)
            return c
        lax.fori_loop(0, tm, body, 0)
        wait_rows(ssem.at[slot], ybuf.at[slot])
        wait_rows(gsem.at[other], xbuf.at[other])


def _moe(u2, buf_tok, sdest, block_e, w1b, w3b, w2b, layer, n_out_rows):
    T, D = u2.shape
    _, E, _, FF = w1b.shape
    nb = buf_tok.shape[0]
    tm = buf_tok.shape[2]
    idx_spec = lambda f: pl.BlockSpec((None, 1, tm), f, memory_space=pltpu.SMEM)
    grid_spec = pltpu.PrefetchScalarGridSpec(
        num_scalar_prefetch=1,
        grid=(nb,),
        in_specs=[
            idx_spec(lambda i, be: (i, 0, 0)),
            idx_spec(lambda i, be: (jnp.minimum(i + 1, nb - 1), 0, 0)),
            idx_spec(lambda i, be: (i, 0, 0)),
            idx_spec(lambda i, be: (i + 1, 0, 0)),
            pl.BlockSpec(memory_space=pl.ANY),
            pl.BlockSpec((None, 1, D, FF), lambda i, be: (layer, be[i], 0, 0)),
            pl.BlockSpec((None, 1, D, FF), lambda i, be: (layer, be[i], 0, 0)),
            pl.BlockSpec((None, 1, FF, D), lambda i, be: (layer, be[i], 0, 0)),
        ],
        out_specs=pl.BlockSpec(memory_space=pl.ANY),
        scratch_shapes=[pltpu.VMEM((2, tm, D), F32), pltpu.VMEM((2, tm, D), F32), pltpu.VMEM((tm, D), BF16),
                        pltpu.SemaphoreType.DMA((2,)), pltpu.SemaphoreType.DMA((2,))],
    )
    return pl.pallas_call(
        _moe_kernel,
        grid_spec=grid_spec,
        out_shape=jax.ShapeDtypeStruct((n_out_rows, D), F32),
        compiler_params=_cparams(("arbitrary",)),
        name="moe",
    )(block_e, buf_tok, buf_tok, sdest, sdest, u2, w1b, w3b, w2b)


def _combine_kernel(y0_ref, y1_ref, x1_ref, w_ref, ada_ref, g_ref, b_ref, o_ref, *, alpha):
    g2 = ada_ref[0, 5:6, :]
    w = w_ref[...]
    h2 = w[:, 0:1] * y0_ref[...] + w[:, 1:2] * y1_ref[...]
    o_ref[...] = _ln(alpha * x1_ref[...] + g2 * h2) * g_ref[...] + b_ref[...]


def _combine(y_tok, x1, wts, ada_l, g2v, b2v, S, alpha):
    T, D = x1.shape
    tm = min(TM_COMB, S)
    spb = S // tm
    nb = T // tm
    const = lambda a: pl.BlockSpec(a.shape, lambda i: (0,) * a.ndim)
    return pl.pallas_call(
        functools.partial(_combine_kernel, alpha=alpha),
        grid=(nb,),
        in_specs=[
            pl.BlockSpec((tm, D), lambda i: (i, 0)),
            pl.BlockSpec((tm, D), lambda i: (i + nb, 0)),
            pl.BlockSpec((tm, D), lambda i: (i, 0)),
            pl.BlockSpec((tm, TOP_K), lambda i: (i, 0)),
            pl.BlockSpec((1, 6, D), lambda i: (i // spb, 0, 0)),
            const(g2v), const(b2v),
        ],
        out_specs=pl.BlockSpec((tm, D), lambda i: (i, 0)),
        out_shape=jax.ShapeDtypeStruct((T, D), F32),
        compiler_params=_cparams(("arbitrary",)),
        name="combine",
    )(y_tok, y_tok, x1, wts, ada_l, g2v, b2v)


def _plan(ridx, tm):
    _, T = ridx.shape
    A = TOP_K * T
    nb = A // tm + N_EXPERTS
    P = nb * tm
    flat_e = ridx.reshape(A)
    experts = jnp.arange(N_EXPERTS, dtype=jnp.int32)
    counts = jnp.sum((flat_e[:, None] == experts[None, :]).astype(jnp.int32), axis=0)
    start = jnp.cumsum(counts) - counts
    padded = (counts + tm - 1) // tm * tm
    pend = jnp.cumsum(padded)
    pstart = pend - padded
    by_expert = jnp.sort(flat_e * A + jnp.arange(A, dtype=jnp.int32)) % A
    block_starts = jnp.arange(nb, dtype=jnp.int32) * tm
    block_e = jnp.minimum(jnp.sum((pend[None, :] <= block_starts[:, None]).astype(jnp.int32), axis=1),
                          N_EXPERTS - 1).astype(jnp.int32)
    p = jnp.arange(P, dtype=jnp.int32)
    e_p = jnp.repeat(block_e, tm)
    j = p - pstart[e_p]
    valid = j < counts[e_p]
    a_p = by_expert[jnp.clip(start[e_p] + j, 0, A - 1)]
    buf_tok = jnp.where(valid, a_p % T, 0).astype(jnp.int32)
    sdest = jnp.where(valid, a_p, A + p - start[e_p] - counts[e_p]).astype(jnp.int32)
    sdest = jnp.concatenate([P + jnp.arange(tm, dtype=jnp.int32), sdest])
    return buf_tok.reshape(nb, 1, tm), sdest.reshape(nb + 1, 1, tm), block_e, P + tm


def _rotary_tables(S):
    half = RET_DK // 2
    inv = RET_THETA ** (-jnp.arange(half, dtype=F32) / half)
    ang = jnp.arange(S, dtype=F32)[:, None] * inv[None, :]
    cos, sin = jnp.cos(ang), jnp.sin(ang)
    cos_t = jnp.tile(jnp.concatenate([cos, cos], axis=1), (1, LANES // RET_DK))
    sin_t = jnp.tile(jnp.concatenate([-sin, sin], axis=1), (1, LANES // RET_DK))
    return cos_t, sin_t


def kernel(x, c, w_ada, b_ada, w_in, conv_w, conv_b, conv_ln_g, conv_ln_b, b_forget, w_conv_out, w_ret_out,
           w_fox_out, w_gate, b_gate, w_out, ln1_g, ln1_b, w_router, b_router, w1, w3, w2, ln2_g, ln2_b):
    B, S, D = x.shape
    L = w_ada.shape[0]
    T = B * S
    alpha = (2 * L) ** 0.25

    ada = _ada(c, w_ada, b_ada).reshape(L, B, 6, D)
    cos_t, sin_t = _rotary_tables(S)

    wr = jnp.zeros((D, LANES), F32).at[:, :N_EXPERTS].set(w_router)
    wr_hi = wr.astype(BF16)
    wr_lo = (wr - wr_hi.astype(F32)).astype(BF16)
    wr2 = jnp.stack([wr_hi, wr_lo])
    br_b = jnp.broadcast_to(b_router.astype(F32)[:, None], (N_EXPERTS, min(TM_PROJ, S)))
    w1b, w3b, w2b = w1.astype(BF16), w3.astype(BF16), w2.astype(BF16)

    for l in range(L):
        w_in_b = jnp.zeros((D, IN_COLS_PAD), BF16).at[:, :w_in.shape[2]].set(w_in[l].astype(BF16))
        bf_pad = jnp.zeros((1, LANES), F32).at[0, :FOX_HEADS].set(b_forget[l])
        h, rq, rk, rv, rg, fq, fk, fv = _inproj(x, ada[l], w_in_b, cos_t, sin_t, bf_pad)
        conv_act = _conv(h, conv_w[l], conv_b[l], conv_ln_g[l], conv_ln_b[l])
        ret_act = _retention(rq, rk, rv, rg)
        fox_act = _fox(fq, fk, fv)
        x1, u2, ridx, rw = _merge(
            x.reshape(T, D), conv_act.reshape(T, 512), ret_act.reshape(T, 512), fox_act.reshape(T, 512), ada[l],
            w_gate[l].astype(BF16), b_gate[l].reshape(1, -1), w_conv_out[l].astype(BF16), w_ret_out[l].astype(BF16),
            w_fox_out[l].astype(BF16), w_out[l].astype(BF16), ln1_g[l].reshape(1, D), ln1_b[l].reshape(1, D),
            wr2, br_b, S, alpha)
        buf_tok, sdest, block_e, n_out_rows = _plan(ridx, TM_MOE)
        y_tok = _moe(u2, buf_tok, sdest, block_e, w1b, w3b, w2b, l, n_out_rows)
        x = _combine(y_tok, x1, rw.T, ada[l], ln2_g[l].reshape(1, D), ln2_b[l].reshape(1, D), S,
                     alpha).reshape(B, S, D)
    return x
```
